```python
import jax
import jax.numpy as jnp
from jax import lax
import numpy as np

D_MODEL = 1024
BATCH = 2
SEQ = 16384
DEPTH = 1
DEC_BATCH = 16
DEC_SEQ = 2048
PAST_LEN = 128

MIX_WIDTH = D_MODEL
D_CONV_A = MIX_WIDTH // 2
D_CONV_B = MIX_WIDTH - D_CONV_A
KERNEL_A = 3
KERNEL_B = 31
IN_PROJ = 3 * D_CONV_A + 2 * D_CONV_B
N_MEM = 256
N_XHEADS = 4
XHEAD_DIM = D_MODEL // N_XHEADS
D_FF = -(-8 * D_MODEL // (3 * 256)) * 256
RMS_EPS = 1e-6
LN_EPS = 1e-5

kernel_name = "hybrid_conv_groups_encoder"


def rms_norm(x, g):
    xf = x.astype(jnp.float32)
    y = xf * lax.rsqrt(jnp.mean(xf * xf, axis=-1, keepdims=True) + RMS_EPS)
    return (y * g.astype(jnp.float32)).astype(x.dtype)


def layer_norm(x, g, b):
    xf = x.astype(jnp.float32)
    mu = jnp.mean(xf, axis=-1, keepdims=True)
    var = jnp.mean(jnp.square(xf - mu), axis=-1, keepdims=True)
    y = (xf - mu) * lax.rsqrt(var + LN_EPS)
    return (y * g.astype(jnp.float32) + b.astype(jnp.float32)).astype(x.dtype)


def depthwise_conv(x, w):
    k, c = w.shape
    pad = k // 2
    return lax.conv_general_dilated(
        x, w[:, None, :].astype(x.dtype), window_strides=(1,), padding=((pad, pad),),
        dimension_numbers=("NWC", "WIO", "NWC"), feature_group_count=c)


def encoder_layer(x, mem, norm_mix_g, w_in, conv_a_w, conv_b_w, conv_b_bias, ln_b_g, ln_b_b, w_out,
                  norm_xattn_g, norm_mem_g, w_q, w_kv, w_o, norm_ffn_g, w_gate_up, w_down):
    bsz, seq, _ = x.shape
    h = rms_norm(x, norm_mix_g)
    proj = h @ w_in
    a_b, a_c, a_v, b_val, b_gate = jnp.split(
        proj, [D_CONV_A, 2 * D_CONV_A, 3 * D_CONV_A, 3 * D_CONV_A + D_CONV_B], axis=-1)
    y_a = a_b * depthwise_conv(a_c * a_v, conv_a_w)
    u = b_val * jax.nn.sigmoid(b_gate)
    u = depthwise_conv(u, conv_b_w) + conv_b_bias
    y_b = jax.nn.silu(layer_norm(u, ln_b_g, ln_b_b))
    x = x + jnp.concatenate([y_a, y_b], axis=-1) @ w_out
    q = (rms_norm(x, norm_xattn_g) @ w_q).reshape(bsz, seq, N_XHEADS, XHEAD_DIM)
    kv = rms_norm(mem, norm_mem_g) @ w_kv
    k, v = jnp.split(kv, 2, axis=-1)
    k = k.reshape(bsz, N_MEM, N_XHEADS, XHEAD_DIM)
    v = v.reshape(bsz, N_MEM, N_XHEADS, XHEAD_DIM)
    scores = jnp.einsum("bshd,bmhd->bhsm", q, k).astype(jnp.float32) * (XHEAD_DIM ** -0.5)
    p = jax.nn.softmax(scores, axis=-1).astype(v.dtype)
    o = jnp.einsum("bhsm,bmhd->bshd", p, v).reshape(bsz, seq, D_MODEL)
    x = x + o @ w_o
    g, up = jnp.split(rms_norm(x, norm_ffn_g) @ w_gate_up, 2, axis=-1)
    x = x + (jax.nn.silu(g) * up) @ w_down
    return x


def encoder_trunk(x, mem, norm_mix_g, w_in, conv_a_w, conv_b_w, conv_b_bias, ln_b_g, ln_b_b, w_out,
                  norm_xattn_g, norm_mem_g, w_q, w_kv, w_o, norm_ffn_g, w_gate_up, w_down, norm_final_g):
    for l in range(DEPTH):
        x = encoder_layer(x, mem, norm_mix_g[l], w_in[l], conv_a_w[l], conv_b_w[l], conv_b_bias[l],
                          ln_b_g[l], ln_b_b[l], w_out[l], norm_xattn_g[l], norm_mem_g[l], w_q[l],
                          w_kv[l], w_o[l], norm_ffn_g[l], w_gate_up[l], w_down[l])
    return rms_norm(x, norm_final_g)


def setup_inputs(seed: int = 0) -> dict:
    key = jax.random.key(seed)
    ks = jax.random.split(key, 24)
    f32 = jnp.float32
    L = DEPTH

    def normal(k, shape):
        return jax.random.normal(k, shape, f32)

    def dense(k, shape, fan_in):
        return jax.random.normal(k, shape, f32) * (fan_in ** -0.5)

    def gain(k, shape):
        return 1.0 + 0.02 * jax.random.normal(k, shape, f32)

    def bias(k, shape):
        return 0.01 * jax.random.normal(k, shape, f32)

    return {
        "x_prompt": normal(ks[0], (BATCH, SEQ, D_MODEL)),
        "x_sample": normal(ks[1], (DEC_BATCH, DEC_SEQ, D_MODEL)),
        "mem_prompt": normal(ks[2], (BATCH, N_MEM, D_MODEL)),
        "mem_sample": normal(ks[3], (DEC_BATCH, N_MEM, D_MODEL)),
        "norm_mix_g": gain(ks[4], (L, D_MODEL)),
        "w_in": dense(ks[5], (L, D_MODEL, IN_PROJ), D_MODEL),
        "conv_a_w": dense(ks[6], (L, KERNEL_A, D_CONV_A), KERNEL_A),
        "conv_b_w": dense(ks[7], (L, KERNEL_B, D_CONV_B), KERNEL_B),
        "conv_b_bias": bias(ks[8], (L, D_CONV_B)),
        "ln_b_g": gain(ks[9], (L, D_CONV_B)),
        "ln_b_b": bias(ks[10], (L, D_CONV_B)),
        "w_out": dense(ks[11], (L, MIX_WIDTH, D_MODEL), MIX_WIDTH),
        "norm_xattn_g": gain(ks[12], (L, D_MODEL)),
        "norm_mem_g": gain(ks[13], (L, D_MODEL)),
        "w_q": dense(ks[14], (L, D_MODEL, D_MODEL), D_MODEL),
        "w_kv": dense(ks[15], (L, D_MODEL, 2 * D_MODEL), D_MODEL),
        "w_o": dense(ks[16], (L, D_MODEL, D_MODEL), D_MODEL),
        "norm_ffn_g": gain(ks[17], (L, D_MODEL)),
        "w_gate_up": dense(ks[18], (L, D_MODEL, 2 * D_FF), D_MODEL),
        "w_down": dense(ks[19], (L, D_FF, D_MODEL), D_FF),
        "norm_final_g": gain(ks[20], (D_MODEL,)),
    }


def reference(x_prompt, x_sample, mem_prompt, mem_sample, norm_mix_g, w_in, conv_a_w, conv_b_w,
              conv_b_bias, ln_b_g, ln_b_b, w_out, norm_xattn_g, norm_mem_g, w_q, w_kv, w_o,
              norm_ffn_g, w_gate_up, w_down, norm_final_g):
    y_prompt = encoder_trunk(x_prompt, mem_prompt, norm_mix_g, w_in, conv_a_w, conv_b_w, conv_b_bias,
                             ln_b_g, ln_b_b, w_out, norm_xattn_g, norm_mem_g, w_q, w_kv, w_o,
                             norm_ffn_g, w_gate_up, w_down, norm_final_g)
    y_sample = encoder_trunk(x_sample, mem_sample, norm_mix_g, w_in, conv_a_w, conv_b_w, conv_b_bias,
                             ln_b_g, ln_b_b, w_out, norm_xattn_g, norm_mem_g, w_q, w_kv, w_o,
                             norm_ffn_g, w_gate_up, w_down, norm_final_g)
    return (y_prompt, y_sample)
```

```python
import functools

import jax
import jax.numpy as jnp
from jax import lax
from jax.experimental import pallas as pl
from jax.experimental.pallas import tpu as pltpu

RMS_EPS = 1e-6
LN_EPS = 1e-5
KERNEL_A = 3
KERNEL_B = 31
N_XHEADS = 4
HALO = 16
CONV_ROWS = 64
LANE = 128
VMEM_LIMIT = 60 * 1024 * 1024
TILE = 512
FFN_CHUNK = 256

_F32 = jnp.float32
_BF16 = jnp.bfloat16


def _rms(x, g):
    ms = jnp.mean(x * x, axis=-1, keepdims=True)
    return x * lax.rsqrt(ms + RMS_EPS) * g


def _dot(a, b):
    return jnp.dot(a, b, preferred_element_type=_F32)


def _sigmoid(x):
    return 1.0 / (1.0 + jnp.exp(-x))


def _kv_kernel(mem_ref, g_ref, w_ref, kv_ref):
    h = _rms(mem_ref[...], g_ref[...]).astype(_BF16)
    kv_ref[...] = _dot(h, w_ref[...]).astype(_BF16)


def _kv_proj(mem2d, g, w_kv, tile):
    n, d = mem2d.shape
    return pl.pallas_call(
        _kv_kernel,
        grid=(n // tile,),
        in_specs=[
            pl.BlockSpec((tile, d), lambda i: (i, 0)),
            pl.BlockSpec((1, d), lambda i: (0, 0)),
            pl.BlockSpec(w_kv.shape, lambda i: (0, 0)),
        ],
        out_specs=pl.BlockSpec((tile, w_kv.shape[1]), lambda i: (i, 0)),
        out_shape=jax.ShapeDtypeStruct((n, w_kv.shape[1]), _BF16),
        compiler_params=pltpu.CompilerParams(
            dimension_semantics=("arbitrary",), vmem_limit_bytes=VMEM_LIMIT),
        name="kv_proj",
    )(mem2d, g, w_kv)


def _mixer_kernel(xm_ref, xp_ref, xn_ref, g_ref, w_in_ref, ca_ref, cb_ref, cbias_ref,
                  lng_ref, lnb_ref, w_out_ref, o_ref, h_scr, cv_scr, u_scr, y_scr,
                  *, tile, tiles_per_seq, dca, dcb):
    i = pl.program_id(0)
    pos = i % tiles_per_seq
    rows = tile + 2 * HALO

    g = g_ref[...]
    h_scr[0:HALO, :] = _rms(xp_ref[...], g).astype(_BF16)
    h_scr[HALO:HALO + tile, :] = _rms(xm_ref[...], g).astype(_BF16)
    h_scr[HALO + tile:rows, :] = _rms(xn_ref[...], g).astype(_BF16)

    r = lax.broadcasted_iota(jnp.int32, (rows, 1), 0)
    valid = jnp.logical_and(jnp.logical_or(r >= HALO, pos > 0),
                            jnp.logical_or(r < HALO + tile, pos < tiles_per_seq - 1))

    h_all = h_scr[...]
    a_c = _dot(h_all, w_in_ref[:, dca:2 * dca])
    a_v = _dot(h_all, w_in_ref[:, 2 * dca:3 * dca])
    cv_scr[...] = jnp.where(valid, a_c * a_v, 0.0)
    a_b = _dot(h_scr[HALO:HALO + tile, :], w_in_ref[:, 0:dca])
    pad_a = KERNEL_A // 2
    conv_a = jnp.zeros((tile, dca), _F32)
    for k in range(KERNEL_A):
        conv_a = conv_a + ca_ref[k:k + 1, :] * cv_scr[HALO - pad_a + k:HALO - pad_a + k + tile, :]
    y_scr[:, 0:dca] = (a_b * conv_a).astype(_BF16)

    off = 3 * dca
    b_val = _dot(h_all, w_in_ref[:, off:off + dcb])
    b_gate = _dot(h_all, w_in_ref[:, off + dcb:off + 2 * dcb])
    u_scr[...] = jnp.where(valid, b_val * _sigmoid(b_gate), 0.0)

    pad_b = KERNEL_B // 2
    n_chunks = tile // CONV_ROWS

    def conv_chunk(c, carry):
        base = c * CONV_ROWS
        parts = []
        for lc in range(dcb // LANE):
            lanes = slice(lc * LANE, (lc + 1) * LANE)
            acc = jnp.zeros((CONV_ROWS, LANE), _F32)
            for k in range(KERNEL_B):
                acc = acc + cb_ref[k:k + 1, lanes] * u_scr[pl.ds(base + (HALO - pad_b + k), CONV_ROWS), lanes]
            parts.append(acc)
        conv = jnp.concatenate(parts, axis=-1) + cbias_ref[...]
        mu = jnp.mean(conv, axis=-1, keepdims=True)
        cen = conv - mu
        var = jnp.mean(cen * cen, axis=-1, keepdims=True)
        ln = cen * lax.rsqrt(var + LN_EPS) * lng_ref[...] + lnb_ref[...]
        y_scr[pl.ds(base, CONV_ROWS), dca:dca + dcb] = (ln * _sigmoid(ln)).astype(_BF16)
        return carry

    for c in range(n_chunks):
        conv_chunk(c, 0)

    o_ref[...] = xm_ref[...] + _dot(y_scr[...], w_out_ref[...])


def _mixer(x2d, seq, g, w_in, ca, cb, cbias, lng, lnb, w_out, tile):
    n, d = x2d.shape
    dca = ca.shape[1]
    dcb = cb.shape[1]
    tiles_per_seq = seq // tile
    hb = tile // HALO
    n_hb = n // HALO
    kern = functools.partial(_mixer_kernel, tile=tile, tiles_per_seq=tiles_per_seq,
                             dca=dca, dcb=dcb)
    const = lambda i: (0, 0)
    return pl.pallas_call(
        kern,
        grid=(n // tile,),
        in_specs=[
            pl.BlockSpec((tile, d), lambda i: (i, 0)),
            pl.BlockSpec((HALO, d), lambda i: (jnp.maximum(i * hb - 1, 0), 0)),
            pl.BlockSpec((HALO, d), lambda i: (jnp.minimum((i + 1) * hb, n_hb - 1), 0)),
            pl.BlockSpec((1, d), const),
            pl.BlockSpec(w_in.shape, const),
            pl.BlockSpec(ca.shape, const),
            pl.BlockSpec(cb.shape, const),
            pl.BlockSpec((1, dcb), const),
            pl.BlockSpec((1, dcb), const),
            pl.BlockSpec((1, dcb), const),
            pl.BlockSpec(w_out.shape, const),
        ],
        out_specs=pl.BlockSpec((tile, d), lambda i: (i, 0)),
        out_shape=jax.ShapeDtypeStruct((n, d), _F32),
        scratch_shapes=[
            pltpu.VMEM((tile + 2 * HALO, d), _BF16),
            pltpu.VMEM((tile + 2 * HALO, dca), _F32),
            pltpu.VMEM((tile + 2 * HALO, dcb), _F32),
            pltpu.VMEM((tile, dca + dcb), _BF16),
        ],
        compiler_params=pltpu.CompilerParams(
            dimension_semantics=("arbitrary",), vmem_limit_bytes=VMEM_LIMIT),
        name="mixer",
    )(x2d, x2d, x2d, g, w_in, ca, cb, cbias, lng, lnb, w_out)


def _xattn_kernel(x_ref, g_ref, wq_ref, k_ref, v_ref, wo_ref, o_ref, o_scr, *, head_dim):
    x = x_ref[...]
    h = _rms(x, g_ref[...]).astype(_BF16)
    q = (_dot(h, wq_ref[...]) * (head_dim ** -0.5)).astype(_BF16)
    for hd in range(N_XHEADS):
        cols = slice(hd * head_dim, (hd + 1) * head_dim)
        s = lax.dot_general(q[:, cols], k_ref[:, cols], (((1,), (1,)), ((), ())),
                            preferred_element_type=_F32)
        m = jnp.max(s, axis=-1, keepdims=True)
        e = jnp.exp(s - m)
        p = e * (1.0 / jnp.sum(e, axis=-1, keepdims=True))
        o_scr[:, cols] = _dot(p.astype(_BF16), v_ref[:, cols]).astype(_BF16)
    o_ref[...] = x + _dot(o_scr[...], wo_ref[...])


def _xattn(x2d, seq, n_mem, g, w_q, kv, w_o, tile):
    n, d = x2d.shape
    tiles_per_seq = seq // tile
    head_dim = d // N_XHEADS
    assert head_dim & (head_dim - 1) == 0 and head_dim.bit_length() % 2 == 1, head_dim
    kern = functools.partial(_xattn_kernel, head_dim=head_dim)
    const = lambda i: (0, 0)
    return pl.pallas_call(
        kern,
        grid=(n // tile,),
        in_specs=[
            pl.BlockSpec((tile, d), lambda i: (i, 0)),
            pl.BlockSpec((1, d), const),
            pl.BlockSpec(w_q.shape, const),
            pl.BlockSpec((n_mem, d), lambda i: (i // tiles_per_seq, 0)),
            pl.BlockSpec((n_mem, d), lambda i: (i // tiles_per_seq, 1)),
            pl.BlockSpec(w_o.shape, const),
        ],
        out_specs=pl.BlockSpec((tile, d), lambda i: (i, 0)),
        out_shape=jax.ShapeDtypeStruct((n, d), _F32),
        scratch_shapes=[pltpu.VMEM((tile, d), _BF16)],
        compiler_params=pltpu.CompilerParams(
            dimension_semantics=("arbitrary",), vmem_limit_bytes=VMEM_LIMIT),
        name="xattn",
    )(x2d, g, w_q, kv, kv, w_o)


def _ffn_kernel(x_ref, g_ref, wgu_ref, wd_ref, gf_ref, o_ref, a_scr, *, d_ff, chunk):
    x = x_ref[...]
    h = _rms(x, g_ref[...]).astype(_BF16)
    for c in range(d_ff // chunk):
        gate = _dot(h, wgu_ref[:, c * chunk:(c + 1) * chunk])
        up = _dot(h, wgu_ref[:, d_ff + c * chunk:d_ff + (c + 1) * chunk])
        a_scr[:, c * chunk:(c + 1) * chunk] = (gate * _sigmoid(gate) * up).astype(_BF16)
    y = x + _dot(a_scr[...], wd_ref[...])
    o_ref[...] = _rms(y, gf_ref[...])


def _ffn(x2d, g, w_gate_up, w_down, g_final, tile, chunk):
    n, d = x2d.shape
    d_ff = w_down.shape[0]
    kern = functools.partial(_ffn_kernel, d_ff=d_ff, chunk=chunk)
    const = lambda i: (0, 0)
    return pl.pallas_call(
        kern,
        grid=(n // tile,),
        in_specs=[
            pl.BlockSpec((tile, d), lambda i: (i, 0)),
            pl.BlockSpec((1, d), const),
            pl.BlockSpec(w_gate_up.shape, const),
            pl.BlockSpec(w_down.shape, const),
            pl.BlockSpec((1, d), const),
        ],
        out_specs=pl.BlockSpec((tile, d), lambda i: (i, 0)),
        out_shape=jax.ShapeDtypeStruct((n, d), _F32),
        scratch_shapes=[pltpu.VMEM((tile, d_ff), _BF16)],
        compiler_params=pltpu.CompilerParams(
            dimension_semantics=("arbitrary",), vmem_limit_bytes=VMEM_LIMIT),
        name="ffn",
    )(x2d, g, w_gate_up, w_down, g_final)


def _trunk(x, mem, p, tile, ffn_chunk):
    bsz, seq, d = x.shape
    n_mem = mem.shape[1]
    row = lambda v: v.reshape(1, -1)
    x2d = x.reshape(bsz * seq, d)
    kv = _kv_proj(mem.reshape(bsz * n_mem, d), row(p["norm_mem_g"]), p["w_kv"], n_mem)
    x1 = _mixer(x2d, seq, row(p["norm_mix_g"]), p["w_in"], p["conv_a_w"], p["conv_b_w"],
                row(p["conv_b_bias"]), row(p["ln_b_g"]), row(p["ln_b_b"]), p["w_out"], tile)
    x2 = _xattn(x1, seq, n_mem, row(p["norm_xattn_g"]), p["w_q"], kv, p["w_o"], tile)
    y = _ffn(x2, row(p["norm_ffn_g"]), p["w_gate_up"], p["w_down"], row(p["norm_final_g"]),
             tile, ffn_chunk)
    return y.reshape(bsz, seq, d)


def kernel(x_prompt, x_sample, mem_prompt, mem_sample, norm_mix_g, w_in, conv_a_w, conv_b_w,
           conv_b_bias, ln_b_g, ln_b_b, w_out, norm_xattn_g, norm_mem_g, w_q, w_kv, w_o,
           norm_ffn_g, w_gate_up, w_down, norm_final_g):
    assert norm_mix_g.shape[0] == 1, "single-layer trunk"
    tile = min(TILE, x_prompt.shape[1], x_sample.shape[1])
    ffn_chunk = FFN_CHUNK
    p = dict(
        norm_mix_g=norm_mix_g[0], w_in=w_in[0].astype(_BF16), conv_a_w=conv_a_w[0],
        conv_b_w=conv_b_w[0], conv_b_bias=conv_b_bias[0], ln_b_g=ln_b_g[0], ln_b_b=ln_b_b[0],
        w_out=w_out[0].astype(_BF16), norm_xattn_g=norm_xattn_g[0], norm_mem_g=norm_mem_g[0],
        w_q=w_q[0].astype(_BF16), w_kv=w_kv[0].astype(_BF16), w_o=w_o[0].astype(_BF16),
        norm_ffn_g=norm_ffn_g[0], w_gate_up=w_gate_up[0].astype(_BF16),
        w_down=w_down[0].astype(_BF16), norm_final_g=norm_final_g)
    y_prompt = _trunk(x_prompt, mem_prompt, p, tile, ffn_chunk)
    y_sample = _trunk(x_sample, mem_sample, p, tile, ffn_chunk)
    return (y_prompt, y_sample)
```

```python
import functools

import jax
import jax.numpy as jnp
from jax import lax
from jax.experimental import pallas as pl
from jax.experimental.pallas import tpu as pltpu

RMS_EPS = 1e-6
LN_EPS = 1e-5
KERNEL_A = 3
KERNEL_B = 31
N_XHEADS = 4
HALO = 16
CONV_ROWS = 64
LANE = 128
VMEM_LIMIT = 60 * 1024 * 1024
MIXER_TILE = 512
XATTN_TILE = 1024
FFN_TILE = 1024
FFN_CHUNK = 256

_F32 = jnp.float32
_BF16 = jnp.bfloat16


def _rms(x, g):
    ms = jnp.mean(x * x, axis=-1, keepdims=True)
    return x * lax.rsqrt(ms + RMS_EPS) * g


def _dot(a, b):
    return jnp.dot(a, b, preferred_element_type=_F32)


def _sigmoid(x):
    return 1.0 / (1.0 + jnp.exp(-x))


def _resident(shape):
    return pl.BlockSpec(shape, lambda i: (0,) * len(shape), pipeline_mode=pl.Buffered(1))


def _kv_kernel(mem_ref, g_ref, w_ref, kv_ref):
    h = _rms(mem_ref[...], g_ref[...]).astype(_BF16)
    kv_ref[...] = _dot(h, w_ref[...]).astype(_BF16)


def _kv_proj(mem2d, g, w_kv, tile):
    n, d = mem2d.shape
    return pl.pallas_call(
        _kv_kernel,
        grid=(n // tile,),
        in_specs=[
            pl.BlockSpec((tile, d), lambda i: (i, 0)),
            pl.BlockSpec((1, d), lambda i: (0, 0)),
            _resident(w_kv.shape),
        ],
        out_specs=pl.BlockSpec((tile, w_kv.shape[1]), lambda i: (i, 0)),
        out_shape=jax.ShapeDtypeStruct((n, w_kv.shape[1]), _BF16),
        compiler_params=pltpu.CompilerParams(
            dimension_semantics=("arbitrary",), vmem_limit_bytes=VMEM_LIMIT),
        name="kv_proj",
    )(mem2d, g, w_kv)


def _mixer_kernel(xm_ref, xp_ref, xn_ref, g_ref, w_in_ref, ca_ref, cb_ref, cbias_ref,
                  lng_ref, lnb_ref, w_out_ref, o_ref, h_scr, cv_scr, u_scr, y_scr,
                  *, tile, tiles_per_seq, dca, dcb):
    i = pl.program_id(0)
    pos = i % tiles_per_seq
    rows = tile + 2 * HALO

    g = g_ref[...]
    h_scr[0:HALO, :] = _rms(xp_ref[...], g).astype(_BF16)
    h_scr[HALO:HALO + tile, :] = _rms(xm_ref[...], g).astype(_BF16)
    h_scr[HALO + tile:rows, :] = _rms(xn_ref[...], g).astype(_BF16)

    keep_top = pos > 0
    keep_bot = pos < tiles_per_seq - 1

    def store_slabs(scr, val):
        for c in range(val.shape[1] // LANE):
            blk = val[:, c * LANE:(c + 1) * LANE]
            scr[c, 0:HALO, :] = jnp.where(keep_top, blk[0:HALO], 0.0)
            scr[c, HALO:HALO + tile, :] = blk[HALO:HALO + tile]
            scr[c, HALO + tile:rows, :] = jnp.where(keep_bot, blk[HALO + tile:rows], 0.0)

    h_all = h_scr[...]
    a_c = _dot(h_all, w_in_ref[:, dca:2 * dca])
    a_v = _dot(h_all, w_in_ref[:, 2 * dca:3 * dca])
    store_slabs(cv_scr, a_c * a_v)
    a_b = _dot(h_scr[HALO:HALO + tile, :], w_in_ref[:, 0:dca])
    pad_a = KERNEL_A // 2
    for lc in range(dca // LANE):
        lanes = slice(lc * LANE, (lc + 1) * LANE)
        conv_a = jnp.zeros((tile, LANE), _F32)
        for k in range(KERNEL_A):
            conv_a = conv_a + ca_ref[k:k + 1, lanes] * cv_scr[lc, pl.ds(HALO - pad_a + k, tile), :]
        y_scr[:, lanes] = (a_b[:, lanes] * conv_a).astype(_BF16)

    off = 3 * dca
    b_val = _dot(h_all, w_in_ref[:, off:off + dcb])
    b_gate = _dot(h_all, w_in_ref[:, off + dcb:off + 2 * dcb])
    store_slabs(u_scr, b_val * _sigmoid(b_gate))

    pad_b = KERNEL_B // 2
    for c in range(tile // CONV_ROWS):
        base = c * CONV_ROWS
        parts = []
        for lc in range(dcb // LANE):
            lanes = slice(lc * LANE, (lc + 1) * LANE)
            acc = jnp.zeros((CONV_ROWS, LANE), _F32)
            for k in range(KERNEL_B):
                acc = acc + cb_ref[k:k + 1, lanes] * u_scr[lc, pl.ds(base + HALO - pad_b + k, CONV_ROWS), :]
            parts.append(acc)
        conv = jnp.concatenate(parts, axis=-1) + cbias_ref[...]
        mu = jnp.mean(conv, axis=-1, keepdims=True)
        cen = conv - mu
        var = jnp.mean(cen * cen, axis=-1, keepdims=True)
        ln = cen * lax.rsqrt(var + LN_EPS) * lng_ref[...] + lnb_ref[...]
        y_scr[base:base + CONV_ROWS, dca:dca + dcb] = (ln * _sigmoid(ln)).astype(_BF16)

    o_ref[...] = xm_ref[...] + _dot(y_scr[...], w_out_ref[...])


def _mixer(x2d, seq, g, w_in, ca, cb, cbias, lng, lnb, w_out, tile):
    n, d = x2d.shape
    dca = ca.shape[1]
    dcb = cb.shape[1]
    tiles_per_seq = seq // tile
    hb = tile // HALO
    n_hb = n // HALO
    kern = functools.partial(_mixer_kernel, tile=tile, tiles_per_seq=tiles_per_seq,
                             dca=dca, dcb=dcb)
    const = lambda i: (0, 0)
    return pl.pallas_call(
        kern,
        grid=(n // tile,),
        in_specs=[
            pl.BlockSpec((tile, d), lambda i: (i, 0)),
            pl.BlockSpec((HALO, d), lambda i: (jnp.maximum(i * hb - 1, 0), 0)),
            pl.BlockSpec((HALO, d), lambda i: (jnp.minimum((i + 1) * hb, n_hb - 1), 0)),
            pl.BlockSpec((1, d), const),
            _resident(w_in.shape),
            pl.BlockSpec(ca.shape, const),
            pl.BlockSpec(cb.shape, const),
            pl.BlockSpec((1, dcb), const),
            pl.BlockSpec((1, dcb), const),
            pl.BlockSpec((1, dcb), const),
            _resident(w_out.shape),
        ],
        out_specs=pl.BlockSpec((tile, d), lambda i: (i, 0)),
        out_shape=jax.ShapeDtypeStruct((n, d), _F32),
        scratch_shapes=[
            pltpu.VMEM((tile + 2 * HALO, d), _BF16),
            pltpu.VMEM((dca // LANE, tile + 2 * HALO, LANE), _F32),
            pltpu.VMEM((dcb // LANE, tile + 2 * HALO, LANE), _F32),
            pltpu.VMEM((tile, dca + dcb), _BF16),
        ],
        compiler_params=pltpu.CompilerParams(
            dimension_semantics=("arbitrary",), vmem_limit_bytes=VMEM_LIMIT),
        name="mixer",
    )(x2d, x2d, x2d, g, w_in, ca, cb, cbias, lng, lnb, w_out)


def _xattn_kernel(x_ref, g_ref, wq_ref, k_ref, v_ref, wo_ref, o_ref, o_scr, *, head_dim):
    x = x_ref[...]
    h = _rms(x, g_ref[...]).astype(_BF16)
    q = (_dot(h, wq_ref[...]) * (head_dim ** -0.5)).astype(_BF16)
    for hd in range(N_XHEADS):
        cols = slice(hd * head_dim, (hd + 1) * head_dim)
        s = lax.dot_general(q[:, cols], k_ref[:, cols], (((1,), (1,)), ((), ())),
                            preferred_element_type=_F32)
        m = jnp.max(s, axis=-1, keepdims=True)
        e = jnp.exp(s - m)
        p = e * (1.0 / jnp.sum(e, axis=-1, keepdims=True))
        o_scr[:, cols] = _dot(p.astype(_BF16), v_ref[:, cols]).astype(_BF16)
    o_ref[...] = x + _dot(o_scr[...], wo_ref[...])


def _xattn(x2d, seq, n_mem, g, w_q, kv, w_o, tile):
    n, d = x2d.shape
    tiles_per_seq = seq // tile
    head_dim = d // N_XHEADS
    assert head_dim & (head_dim - 1) == 0 and head_dim.bit_length() % 2 == 1, head_dim
    kern = functools.partial(_xattn_kernel, head_dim=head_dim)
    const = lambda i: (0, 0)
    return pl.pallas_call(
        kern,
        grid=(n // tile,),
        in_specs=[
            pl.BlockSpec((tile, d), lambda i: (i, 0)),
            pl.BlockSpec((1, d), const),
            _resident(w_q.shape),
            pl.BlockSpec((n_mem, d), lambda i: (i // tiles_per_seq, 0)),
            pl.BlockSpec((n_mem, d), lambda i: (i // tiles_per_seq, 1)),
            _resident(w_o.shape),
        ],
        out_specs=pl.BlockSpec((tile, d), lambda i: (i, 0)),
        out_shape=jax.ShapeDtypeStruct((n, d), _F32),
        scratch_shapes=[pltpu.VMEM((tile, d), _BF16)],
        compiler_params=pltpu.CompilerParams(
            dimension_semantics=("arbitrary",), vmem_limit_bytes=VMEM_LIMIT),
        name="xattn",
    )(x2d, g, w_q, kv, kv, w_o)


def _ffn_kernel(x_ref, g_ref, wgu_ref, wd_ref, gf_ref, o_ref, a_scr, *, d_ff, chunk):
    x = x_ref[...]
    h = _rms(x, g_ref[...]).astype(_BF16)
    for c in range(d_ff // chunk):
        gate = _dot(h, wgu_ref[:, c * chunk:(c + 1) * chunk])
        up = _dot(h, wgu_ref[:, d_ff + c * chunk:d_ff + (c + 1) * chunk])
        a_scr[:, c * chunk:(c + 1) * chunk] = (gate * _sigmoid(gate) * up).astype(_BF16)
    y = x + _dot(a_scr[...], wd_ref[...])
    o_ref[...] = _rms(y, gf_ref[...])


def _ffn(x2d, g, w_gate_up, w_down, g_final, tile, chunk):
    n, d = x2d.shape
    d_ff = w_down.shape[0]
    kern = functools.partial(_ffn_kernel, d_ff=d_ff, chunk=chunk)
    const = lambda i: (0, 0)
    return pl.pallas_call(
        kern,
        grid=(n // tile,),
        in_specs=[
            pl.BlockSpec((tile, d), lambda i: (i, 0)),
            pl.BlockSpec((1, d), const),
            _resident(w_gate_up.shape),
            _resident(w_down.shape),
            pl.BlockSpec((1, d), const),
        ],
        out_specs=pl.BlockSpec((tile, d), lambda i: (i, 0)),
        out_shape=jax.ShapeDtypeStruct((n, d), _F32),
        scratch_shapes=[pltpu.VMEM((tile, d_ff), _BF16)],
        compiler_params=pltpu.CompilerParams(
            dimension_semantics=("arbitrary",), vmem_limit_bytes=VMEM_LIMIT),
        name="ffn",
    )(x2d, g, w_gate_up, w_down, g_final)


def _trunk(x, mem, p):
    bsz, seq, d = x.shape
    n_mem = mem.shape[1]
    row = lambda v: v.reshape(1, -1)
    x2d = x.reshape(bsz * seq, d)
    kv = _kv_proj(mem.reshape(bsz * n_mem, d), row(p["norm_mem_g"]), p["w_kv"], n_mem)
    x1 = _mixer(x2d, seq, row(p["norm_mix_g"]), p["w_in"], p["conv_a_w"], p["conv_b_w"],
                row(p["conv_b_bias"]), row(p["ln_b_g"]), row(p["ln_b_b"]), p["w_out"],
                min(MIXER_TILE, seq))
    x2 = _xattn(x1, seq, n_mem, row(p["norm_xattn_g"]), p["w_q"], kv, p["w_o"],
                min(XATTN_TILE, seq))
    y = _ffn(x2, row(p["norm_ffn_g"]), p["w_gate_up"], p["w_down"], row(p["norm_final_g"]),
             min(FFN_TILE, seq), FFN_CHUNK)
    return y.reshape(bsz, seq, d)


def kernel(x_prompt, x_sample, mem_prompt, mem_sample, norm_mix_g, w_in, conv_a_w, conv_b_w,
           conv_b_bias, ln_b_g, ln_b_b, w_out, norm_xattn_g, norm_mem_g, w_q, w_kv, w_o,
           norm_ffn_g, w_gate_up, w_down, norm_final_g):
    assert norm_mix_g.shape[0] == 1, "single-layer trunk"
    p = dict(
        norm_mix_g=norm_mix_g[0], w_in=w_in[0].astype(_BF16), conv_a_w=conv_a_w[0],
        conv_b_w=conv_b_w[0], conv_b_bias=conv_b_bias[0], ln_b_g=ln_b_g[0], ln_b_b=ln_b_b[0],
        w_out=w_out[0].astype(_BF16), norm_xattn_g=norm_xattn_g[0], norm_mem_g=norm_mem_g[0],
        w_q=w_q[0].astype(_BF16), w_kv=w_kv[0].astype(_BF16), w_o=w_o[0].astype(_BF16),
        norm_ffn_g=norm_ffn_g[0], w_gate_up=w_gate_up[0].astype(_BF16),
        w_down=w_down[0].astype(_BF16), norm_final_g=norm_final_g)
    y_prompt = _trunk(x_prompt, mem_prompt, p)
    y_sample = _trunk(x_sample, mem_sample, p)
    return (y_prompt, y_sample)
```

```python
import functools

import jax
import jax.numpy as jnp
from jax import lax
from jax.experimental import pallas as pl
from jax.experimental.pallas import tpu as pltpu

RMS_EPS = 1e-6
LN_EPS = 1e-5
KERNEL_A = 3
KERNEL_B = 31
N_XHEADS = 4
HALO = 16
CONV_ROWS = 64
LANE = 128
MXU_COLS = 256
VMEM_LIMIT = 60 * 1024 * 1024
MIXER_TILE = 512
XATTN_TILE = 1024
FFN_TILE = 1024
FFN_CHUNK = 256

_F32 = jnp.float32
_BF16 = jnp.bfloat16


def _rms(x, g):
    ms = jnp.mean(x * x, axis=-1, keepdims=True)
    return x * lax.rsqrt(ms + RMS_EPS) * g


def _dot(a, b):
    return jnp.dot(a, b, preferred_element_type=_F32)


def _sigmoid(x):
    return 1.0 / (1.0 + jnp.exp(-x))


def _resident(shape):
    return pl.BlockSpec(shape, lambda i: (0,) * len(shape), pipeline_mode=pl.Buffered(1))


def _kv_kernel(mem_ref, g_ref, w_ref, kv_ref):
    h = _rms(mem_ref[...], g_ref[...]).astype(_BF16)
    kv_ref[...] = _dot(h, w_ref[...]).astype(_BF16)


def _kv_proj(mem2d, g, w_kv, tile):
    n, d = mem2d.shape
    return pl.pallas_call(
        _kv_kernel,
        grid=(n // tile,),
        in_specs=[
            pl.BlockSpec((tile, d), lambda i: (i, 0)),
            pl.BlockSpec((1, d), lambda i: (0, 0)),
            _resident(w_kv.shape),
        ],
        out_specs=pl.BlockSpec((tile, w_kv.shape[1]), lambda i: (i, 0)),
        out_shape=jax.ShapeDtypeStruct((n, w_kv.shape[1]), _BF16),
        compiler_params=pltpu.CompilerParams(
            dimension_semantics=("arbitrary",), vmem_limit_bytes=VMEM_LIMIT),
        name="kv_proj",
    )(mem2d, g, w_kv)


def _mixer_kernel(xm_ref, xp_ref, xn_ref, g_ref, w_in_ref, ca_ref, cb_ref, cbias_ref,
                  lng_ref, lnb_ref, w_out_ref, o_ref,
                  h_scr, cv_scr, u_scr, ab_scr, conv_scr, y_scr,
                  *, tile, tiles_per_seq, dca, dcb):
    i = pl.program_id(0)
    pos = i % tiles_per_seq
    rows = tile + 2 * HALO
    pad_a = KERNEL_A // 2
    pad_b = KERNEL_B // 2
    n_slab = dcb // LANE
    assert dca == dcb, "the emission schedule below pairs group-A and group-B slabs"

    g = g_ref[...]
    h_scr[0:HALO, :] = _rms(xp_ref[...], g).astype(_BF16)
    h_scr[HALO:HALO + tile, :] = _rms(xm_ref[...], g).astype(_BF16)
    h_scr[HALO + tile:rows, :] = _rms(xn_ref[...], g).astype(_BF16)

    keep_top = pos > 0
    keep_bot = pos < tiles_per_seq - 1

    def store_slab(scr, c, blk):
        scr[c, 0:HALO, :] = jnp.where(keep_top, blk[0:HALO], 0.0)
        scr[c, HALO:HALO + tile, :] = blk[HALO:HALO + tile]
        scr[c, HALO + tile:rows, :] = jnp.where(keep_bot, blk[HALO + tile:rows], 0.0)

    def project_b(j):
        r = _dot(h_scr[...], w_in_ref[j])
        store_slab(u_scr, j, r[:, 0:LANE] * _sigmoid(r[:, LANE:2 * LANE]))

    def project_cv(j):
        r = _dot(h_scr[...], w_in_ref[n_slab + j])
        store_slab(cv_scr, j, r[:, 0:LANE] * r[:, LANE:2 * LANE])

    def project_ab(j):
        ab_scr[:, j * MXU_COLS:(j + 1) * MXU_COLS] = _dot(h_scr[HALO:HALO + tile, :],
                                                         w_in_ref[2 * n_slab + j])

    def conv_b_slab(lc):
        lanes = slice(lc * LANE, (lc + 1) * LANE)
        for c in range(tile // CONV_ROWS):
            base = c * CONV_ROWS
            acc = jnp.zeros((CONV_ROWS, LANE), _F32)
            for k in range(KERNEL_B):
                acc = acc + cb_ref[k:k + 1, lanes] * u_scr[lc, pl.ds(base + HALO - pad_b + k, CONV_ROWS), :]
            conv_scr[base:base + CONV_ROWS, lanes] = acc + cbias_ref[:, lanes]

    def conv_a_slab(lc):
        lanes = slice(lc * LANE, (lc + 1) * LANE)
        conv_a = jnp.zeros((tile, LANE), _F32)
        for k in range(KERNEL_A):
            conv_a = conv_a + ca_ref[k:k + 1, lanes] * cv_scr[lc, pl.ds(HALO - pad_a + k, tile), :]
        y_scr[:, lanes] = (ab_scr[:, lanes] * conv_a).astype(_BF16)

    slabs_per_ab = MXU_COLS // LANE
    project_b(0)
    for j in range(n_slab):
        project_cv(j)
        if j % slabs_per_ab == 0:
            project_ab(j // slabs_per_ab)
        if j + 1 < n_slab:
            project_b(j + 1)
        conv_b_slab(j)
        conv_a_slab(j)

    for c in range(tile // CONV_ROWS):
        base = c * CONV_ROWS
        conv = conv_scr[base:base + CONV_ROWS, :]
        mu = jnp.mean(conv, axis=-1, keepdims=True)
        cen = conv - mu
        var = jnp.mean(cen * cen, axis=-1, keepdims=True)
        ln = cen * lax.rsqrt(var + LN_EPS) * lng_ref[...] + lnb_ref[...]
        y_scr[base:base + CONV_ROWS, dca:dca + dcb] = (ln * _sigmoid(ln)).astype(_BF16)

    o_ref[...] = xm_ref[...] + _dot(y_scr[...], w_out_ref[...])


def _pack_w_in(w_in, dca, dcb):
    d = w_in.shape[0]

    def pair(a, b):
        n = a.shape[1] // LANE
        return jnp.stack([a.reshape(d, n, LANE), b.reshape(d, n, LANE)], axis=2).reshape(d, -1)

    a_b, a_c, a_v = w_in[:, 0:dca], w_in[:, dca:2 * dca], w_in[:, 2 * dca:3 * dca]
    b_val, b_gate = w_in[:, 3 * dca:3 * dca + dcb], w_in[:, 3 * dca + dcb:3 * dca + 2 * dcb]
    packed = jnp.concatenate([pair(b_val, b_gate), pair(a_c, a_v), a_b], axis=1)
    return packed.reshape(d, -1, MXU_COLS).transpose(1, 0, 2)


def _mixer(x2d, seq, g, w_in, ca, cb, cbias, lng, lnb, w_out, tile):
    n, d = x2d.shape
    dca = ca.shape[1]
    dcb = cb.shape[1]
    tiles_per_seq = seq // tile
    hb = tile // HALO
    n_hb = n // HALO
    kern = functools.partial(_mixer_kernel, tile=tile, tiles_per_seq=tiles_per_seq,
                             dca=dca, dcb=dcb)
    const = lambda i: (0, 0)
    return pl.pallas_call(
        kern,
        grid=(n // tile,),
        in_specs=[
            pl.BlockSpec((tile, d), lambda i: (i, 0)),
            pl.BlockSpec((HALO, d), lambda i: (jnp.maximum(i * hb - 1, 0), 0)),
            pl.BlockSpec((HALO, d), lambda i: (jnp.minimum((i + 1) * hb, n_hb - 1), 0)),
            pl.BlockSpec((1, d), const),
            _resident(w_in.shape),
            pl.BlockSpec(ca.shape, const),
            pl.BlockSpec(cb.shape, const),
            pl.BlockSpec((1, dcb), const),
            pl.BlockSpec((1, dcb), const),
            pl.BlockSpec((1, dcb), const),
            _resident(w_out.shape),
        ],
        out_specs=pl.BlockSpec((tile, d), lambda i: (i, 0)),
        out_shape=jax.ShapeDtypeStruct((n, d), _F32),
        scratch_shapes=[
            pltpu.VMEM((tile + 2 * HALO, d), _BF16),
            pltpu.VMEM((dca // LANE, tile + 2 * HALO, LANE), _F32),
            pltpu.VMEM((dcb // LANE, tile + 2 * HALO, LANE), _F32),
            pltpu.VMEM((tile, dca), _F32),
            pltpu.VMEM((tile, dcb), _F32),
            pltpu.VMEM((tile, dca + dcb), _BF16),
        ],
        compiler_params=pltpu.CompilerParams(
            dimension_semantics=("arbitrary",), vmem_limit_bytes=VMEM_LIMIT),
        name="mixer",
    )(x2d, x2d, x2d, g, w_in, ca, cb, cbias, lng, lnb, w_out)


def _xattn_kernel(x_ref, g_ref, wq_ref, k_ref, v_ref, wo_ref, o_ref, o_scr, *, head_dim):
    x = x_ref[...]
    h = _rms(x, g_ref[...]).astype(_BF16)
    q = (_dot(h, wq_ref[...]) * (head_dim ** -0.5)).astype(_BF16)
    for hd in range(N_XHEADS):
        cols = slice(hd * head_dim, (hd + 1) * head_dim)
        s = lax.dot_general(q[:, cols], k_ref[:, cols], (((1,), (1,)), ((), ())),
                            preferred_element_type=_F32)
        m = jnp.max(s, axis=-1, keepdims=True)
        e = jnp.exp(s - m)
        inv = 1.0 / jnp.sum(e, axis=-1, keepdims=True)
        o_scr[:, cols] = (_dot(e.astype(_BF16), v_ref[:, cols]) * inv).astype(_BF16)
    o_ref[...] = x + _dot(o_scr[...], wo_ref[...])


def _xattn(x2d, seq, n_mem, g, w_q, kv, w_o, tile):
    n, d = x2d.shape
    tiles_per_seq = seq // tile
    head_dim = d // N_XHEADS
    assert head_dim & (head_dim - 1) == 0 and head_dim.bit_length() % 2 == 1, head_dim
    kern = functools.partial(_xattn_kernel, head_dim=head_dim)
    const = lambda i: (0, 0)
    return pl.pallas_call(
        kern,
        grid=(n // tile,),
        in_specs=[
            pl.BlockSpec((tile, d), lambda i: (i, 0)),
            pl.BlockSpec((1, d), const),
            _resident(w_q.shape),
            pl.BlockSpec((n_mem, d), lambda i: (i // tiles_per_seq, 0)),
            pl.BlockSpec((n_mem, d), lambda i: (i // tiles_per_seq, 1)),
            _resident(w_o.shape),
        ],
        out_specs=pl.BlockSpec((tile, d), lambda i: (i, 0)),
        out_shape=jax.ShapeDtypeStruct((n, d), _F32),
        scratch_shapes=[pltpu.VMEM((tile, d), _BF16)],
        compiler_params=pltpu.CompilerParams(
            dimension_semantics=("arbitrary",), vmem_limit_bytes=VMEM_LIMIT),
        name="xattn",
    )(x2d, g, w_q, kv, kv, w_o)


def _ffn_kernel(x_ref, g_ref, wgu_ref, wd_ref, gf_ref, o_ref, a_scr, *, d_ff, chunk):
    x = x_ref[...]
    h = _rms(x, g_ref[...]).astype(_BF16)
    for c in range(d_ff // chunk):
        gate = _dot(h, wgu_ref[:, c * chunk:(c + 1) * chunk])
        up = _dot(h, wgu_ref[:, d_ff + c * chunk:d_ff + (c + 1) * chunk])
        a_scr[:, c * chunk:(c + 1) * chunk] = (gate * _sigmoid(gate) * up).astype(_BF16)
    y = x + _dot(a_scr[...], wd_ref[...])
    o_ref[...] = _rms(y, gf_ref[...])


def _ffn(x2d, g, w_gate_up, w_down, g_final, tile, chunk):
    n, d = x2d.shape
    d_ff = w_down.shape[0]
    kern = functools.partial(_ffn_kernel, d_ff=d_ff, chunk=chunk)
    const = lambda i: (0, 0)
    return pl.pallas_call(
        kern,
        grid=(n // tile,),
        in_specs=[
            pl.BlockSpec((tile, d), lambda i: (i, 0)),
            pl.BlockSpec((1, d), const),
            _resident(w_gate_up.shape),
            _resident(w_down.shape),
            pl.BlockSpec((1, d), const),
        ],
        out_specs=pl.BlockSpec((tile, d), lambda i: (i, 0)),
        out_shape=jax.ShapeDtypeStruct((n, d), _F32),
        scratch_shapes=[pltpu.VMEM((tile, d_ff), _BF16)],
        compiler_params=pltpu.CompilerParams(
            dimension_semantics=("arbitrary",), vmem_limit_bytes=VMEM_LIMIT),
        name="ffn",
    )(x2d, g, w_gate_up, w_down, g_final)


def _trunk(x, mem, p):
    bsz, seq, d = x.shape
    n_mem = mem.shape[1]
    row = lambda v: v.reshape(1, -1)
    x2d = x.reshape(bsz * seq, d)
    kv = _kv_proj(mem.reshape(bsz * n_mem, d), row(p["norm_mem_g"]), p["w_kv"], n_mem)
    x1 = _mixer(x2d, seq, row(p["norm_mix_g"]), p["w_in"], p["conv_a_w"], p["conv_b_w"],
                row(p["conv_b_bias"]), row(p["ln_b_g"]), row(p["ln_b_b"]), p["w_out"],
                min(MIXER_TILE, seq))
    x2 = _xattn(x1, seq, n_mem, row(p["norm_xattn_g"]), p["w_q"], kv, p["w_o"],
                min(XATTN_TILE, seq))
    y = _ffn(x2, row(p["norm_ffn_g"]), p["w_gate_up"], p["w_down"], row(p["norm_final_g"]),
             min(FFN_TILE, seq), FFN_CHUNK)
    return y.reshape(bsz, seq, d)


def kernel(x_prompt, x_sample, mem_prompt, mem_sample, norm_mix_g, w_in, conv_a_w, conv_b_w,
           conv_b_bias, ln_b_g, ln_b_b, w_out, norm_xattn_g, norm_mem_g, w_q, w_kv, w_o,
           norm_ffn_g, w_gate_up, w_down, norm_final_g):
    assert norm_mix_g.shape[0] == 1, "single-layer trunk"
    p = dict(
        norm_mix_g=norm_mix_g[0],
        w_in=_pack_w_in(w_in[0], conv_a_w.shape[2], conv_b_w.shape[2]).astype(_BF16),
        conv_a_w=conv_a_w[0],
        conv_b_w=conv_b_w[0], conv_b_bias=conv_b_bias[0], ln_b_g=ln_b_g[0], ln_b_b=ln_b_b[0],
        w_out=w_out[0].astype(_BF16), norm_xattn_g=norm_xattn_g[0], norm_mem_g=norm_mem_g[0],
        w_q=w_q[0].astype(_BF16), w_kv=w_kv[0].astype(_BF16), w_o=w_o[0].astype(_BF16),
        norm_ffn_g=norm_ffn_g[0], w_gate_up=w_gate_up[0].astype(_BF16),
        w_down=w_down[0].astype(_BF16), norm_final_g=norm_final_g)
    y_prompt = _trunk(x_prompt, mem_prompt, p)
    y_sample = _trunk(x_sample, mem_sample, p)
    return (y_prompt, y_sample)
```

```python
import functools

import jax
import jax.numpy as jnp
from jax import lax
from jax.experimental import pallas as pl
from jax.experimental.pallas import tpu as pltpu

RMS_EPS = 1e-6
LN_EPS = 1e-5
KERNEL_A = 3
KERNEL_B = 31
N_XHEADS = 4
HALO = 16
CONV_ROWS = 64
LANE = 128
MXU_COLS = 256
VMEM_LIMIT = 60 * 1024 * 1024
MIXER_TILE = 1024
XATTN_TILE = 1024
FFN_TILE = 1024
FFN_CHUNK = 256

_F32 = jnp.float32
_BF16 = jnp.bfloat16


def _rms(x, g):
    ms = jnp.mean(x * x, axis=-1, keepdims=True)
    return x * lax.rsqrt(ms + RMS_EPS) * g


def _dot(a, b):
    return jnp.dot(a, b, preferred_element_type=_F32)


def _sigmoid(x):
    return 1.0 / (1.0 + jnp.exp(-x))


def _resident(shape):
    return pl.BlockSpec(shape, lambda i: (0,) * len(shape), pipeline_mode=pl.Buffered(1))


def _kv_kernel(mem_ref, g_ref, w_ref, kv_ref):
    h = _rms(mem_ref[...], g_ref[...]).astype(_BF16)
    kv_ref[...] = _dot(h, w_ref[...]).astype(_BF16)


def _kv_proj(mem2d, g, w_kv, tile):
    n, d = mem2d.shape
    return pl.pallas_call(
        _kv_kernel,
        grid=(n // tile,),
        in_specs=[
            pl.BlockSpec((tile, d), lambda i: (i, 0)),
            pl.BlockSpec((1, d), lambda i: (0, 0)),
            _resident(w_kv.shape),
        ],
        out_specs=pl.BlockSpec((tile, w_kv.shape[1]), lambda i: (i, 0)),
        out_shape=jax.ShapeDtypeStruct((n, w_kv.shape[1]), _BF16),
        compiler_params=pltpu.CompilerParams(
            dimension_semantics=("arbitrary",), vmem_limit_bytes=VMEM_LIMIT),
        name="kv_proj",
    )(mem2d, g, w_kv)


def _mixer_kernel(xm_ref, xp_ref, xn_ref, g_ref, w_in_ref, ca_ref, cb_ref, cbias_ref,
                  lng_ref, lnb_ref, w_out_ref, o_ref,
                  h_scr, cv_scr, u_scr, ab_scr, conv_scr, y_scr,
                  *, tile, tiles_per_seq, dca, dcb):
    i = pl.program_id(0)
    pos = i % tiles_per_seq
    rows = tile + 2 * HALO
    pad_a = KERNEL_A // 2
    pad_b = KERNEL_B // 2
    n_slab = dcb // LANE
    assert dca == dcb, "the emission schedule below pairs group-A and group-B slabs"

    g = g_ref[...]
    h_scr[0:HALO, :] = _rms(xp_ref[...], g).astype(_BF16)
    h_scr[HALO:HALO + tile, :] = _rms(xm_ref[...], g).astype(_BF16)
    h_scr[HALO + tile:rows, :] = _rms(xn_ref[...], g).astype(_BF16)

    keep_top = pos > 0
    keep_bot = pos < tiles_per_seq - 1

    def store_slab(scr, c, blk):
        scr[c, 0:HALO, :] = jnp.where(keep_top, blk[0:HALO], 0.0)
        scr[c, HALO:HALO + tile, :] = blk[HALO:HALO + tile]
        scr[c, HALO + tile:rows, :] = jnp.where(keep_bot, blk[HALO + tile:rows], 0.0)

    def project_b(j):
        r = _dot(h_scr[...], w_in_ref[j])
        store_slab(u_scr, j, r[:, 0:LANE] * _sigmoid(r[:, LANE:2 * LANE]))

    def project_cv(j):
        r = _dot(h_scr[...], w_in_ref[n_slab + j])
        store_slab(cv_scr, j, r[:, 0:LANE] * r[:, LANE:2 * LANE])

    def project_ab(j):
        ab_scr[:, j * MXU_COLS:(j + 1) * MXU_COLS] = _dot(h_scr[HALO:HALO + tile, :],
                                                         w_in_ref[2 * n_slab + j])

    def conv_b_slab(lc):
        lanes = slice(lc * LANE, (lc + 1) * LANE)
        for c in range(tile // CONV_ROWS):
            base = c * CONV_ROWS
            acc = jnp.zeros((CONV_ROWS, LANE), _F32)
            for k in range(KERNEL_B):
                acc = acc + cb_ref[k:k + 1, lanes] * u_scr[lc, pl.ds(base + HALO - pad_b + k, CONV_ROWS), :]
            conv_scr[base:base + CONV_ROWS, lanes] = acc + cbias_ref[:, lanes]

    def conv_a_slab(lc):
        lanes = slice(lc * LANE, (lc + 1) * LANE)
        conv_a = jnp.zeros((tile, LANE), _F32)
        for k in range(KERNEL_A):
            conv_a = conv_a + ca_ref[k:k + 1, lanes] * cv_scr[lc, pl.ds(HALO - pad_a + k, tile), :]
        y_scr[:, lanes] = (ab_scr[:, lanes] * conv_a).astype(_BF16)

    slabs_per_ab = MXU_COLS // LANE
    project_b(0)
    for j in range(n_slab):
        project_cv(j)
        if j % slabs_per_ab == 0:
            project_ab(j // slabs_per_ab)
        if j + 1 < n_slab:
            project_b(j + 1)
        conv_b_slab(j)
        conv_a_slab(j)

    for c in range(tile // CONV_ROWS):
        base = c * CONV_ROWS
        conv = conv_scr[base:base + CONV_ROWS, :]
        mu = jnp.mean(conv, axis=-1, keepdims=True)
        cen = conv - mu
        var = jnp.mean(cen * cen, axis=-1, keepdims=True)
        ln = cen * lax.rsqrt(var + LN_EPS) * lng_ref[...] + lnb_ref[...]
        y_scr[base:base + CONV_ROWS, dca:dca + dcb] = (ln * _sigmoid(ln)).astype(_BF16)

    o_ref[...] = xm_ref[...] + _dot(y_scr[...], w_out_ref[...])


def _pack_w_in(w_in, dca, dcb):
    d = w_in.shape[0]

    def pair(a, b):
        n = a.shape[1] // LANE
        return jnp.stack([a.reshape(d, n, LANE), b.reshape(d, n, LANE)], axis=2).reshape(d, -1)

    a_b, a_c, a_v = w_in[:, 0:dca], w_in[:, dca:2 * dca], w_in[:, 2 * dca:3 * dca]
    b_val, b_gate = w_in[:, 3 * dca:3 * dca + dcb], w_in[:, 3 * dca + dcb:3 * dca + 2 * dcb]
    packed = jnp.concatenate([pair(b_val, b_gate), pair(a_c, a_v), a_b], axis=1)
    return packed.reshape(d, -1, MXU_COLS).transpose(1, 0, 2)


def _mixer(x2d, seq, g, w_in, ca, cb, cbias, lng, lnb, w_out, tile):
    n, d = x2d.shape
    dca = ca.shape[1]
    dcb = cb.shape[1]
    tiles_per_seq = seq // tile
    hb = tile // HALO
    n_hb = n // HALO
    kern = functools.partial(_mixer_kernel, tile=tile, tiles_per_seq=tiles_per_seq,
                             dca=dca, dcb=dcb)
    const = lambda i: (0, 0)
    return pl.pallas_call(
        kern,
        grid=(n // tile,),
        in_specs=[
            pl.BlockSpec((tile, d), lambda i: (i, 0)),
            pl.BlockSpec((HALO, d), lambda i: (jnp.maximum(i * hb - 1, 0), 0)),
            pl.BlockSpec((HALO, d), lambda i: (jnp.minimum((i + 1) * hb, n_hb - 1), 0)),
            pl.BlockSpec((1, d), const),
            _resident(w_in.shape),
            pl.BlockSpec(ca.shape, const),
            pl.BlockSpec(cb.shape, const),
            pl.BlockSpec((1, dcb), const),
            pl.BlockSpec((1, dcb), const),
            pl.BlockSpec((1, dcb), const),
            _resident(w_out.shape),
        ],
        out_specs=pl.BlockSpec((tile, d), lambda i: (i, 0)),
        out_shape=jax.ShapeDtypeStruct((n, d), _F32),
        scratch_shapes=[
            pltpu.VMEM((tile + 2 * HALO, d), _BF16),
            pltpu.VMEM((dca // LANE, tile + 2 * HALO, LANE), _F32),
            pltpu.VMEM((dcb // LANE, tile + 2 * HALO, LANE), _F32),
            pltpu.VMEM((tile, dca), _F32),
            pltpu.VMEM((tile, dcb), _F32),
            pltpu.VMEM((tile, dca + dcb), _BF16),
        ],
        compiler_params=pltpu.CompilerParams(
            dimension_semantics=("arbitrary",), vmem_limit_bytes=VMEM_LIMIT),
        name="mixer",
    )(x2d, x2d, x2d, g, w_in, ca, cb, cbias, lng, lnb, w_out)


def _xattn_kernel(x_ref, g_ref, wq_ref, k_ref, v_ref, wo_ref, o_ref, o_scr, *, head_dim):
    x = x_ref[...]
    h = _rms(x, g_ref[...]).astype(_BF16)
    q = (_dot(h, wq_ref[...]) * (head_dim ** -0.5)).astype(_BF16)
    for hd in range(N_XHEADS):
        cols = slice(hd * head_dim, (hd + 1) * head_dim)
        s = lax.dot_general(q[:, cols], k_ref[:, cols], (((1,), (1,)), ((), ())),
                            preferred_element_type=_F32)
        m = jnp.max(s, axis=-1, keepdims=True)
        e = jnp.exp(s - m)
        inv = 1.0 / jnp.sum(e, axis=-1, keepdims=True)
        o_scr[:, cols] = (_dot(e.astype(_BF16), v_ref[:, cols]) * inv).astype(_BF16)
    o_ref[...] = x + _dot(o_scr[...], wo_ref[...])


def _xattn(x2d, seq, n_mem, g, w_q, kv, w_o, tile):
    n, d = x2d.shape
    tiles_per_seq = seq // tile
    head_dim = d // N_XHEADS
    assert head_dim & (head_dim - 1) == 0 and head_dim.bit_length() % 2 == 1, head_dim
    kern = functools.partial(_xattn_kernel, head_dim=head_dim)
    const = lambda i: (0, 0)
    return pl.pallas_call(
        kern,
        grid=(n // tile,),
        in_specs=[
            pl.BlockSpec((tile, d), lambda i: (i, 0)),
            pl.BlockSpec((1, d), const),
            _resident(w_q.shape),
            pl.BlockSpec((n_mem, d), lambda i: (i // tiles_per_seq, 0)),
            pl.BlockSpec((n_mem, d), lambda i: (i // tiles_per_seq, 1)),
            _resident(w_o.shape),
        ],
        out_specs=pl.BlockSpec((tile, d), lambda i: (i, 0)),
        out_shape=jax.ShapeDtypeStruct((n, d), _F32),
        scratch_shapes=[pltpu.VMEM((tile, d), _BF16)],
        compiler_params=pltpu.CompilerParams(
            dimension_semantics=("arbitrary",), vmem_limit_bytes=VMEM_LIMIT),
        name="xattn",
    )(x2d, g, w_q, kv, kv, w_o)


def _ffn_kernel(x_ref, g_ref, wgu_ref, wd_ref, gf_ref, o_ref, a_scr, *, d_ff, chunk):
    x = x_ref[...]
    h = _rms(x, g_ref[...]).astype(_BF16)
    for c in range(d_ff // chunk):
        gate = _dot(h, wgu_ref[:, c * chunk:(c + 1) * chunk])
        up = _dot(h, wgu_ref[:, d_ff + c * chunk:d_ff + (c + 1) * chunk])
        a_scr[:, c * chunk:(c + 1) * chunk] = (gate * _sigmoid(gate) * up).astype(_BF16)
    y = x + _dot(a_scr[...], wd_ref[...])
    o_ref[...] = _rms(y, gf_ref[...])


def _ffn(x2d, g, w_gate_up, w_down, g_final, tile, chunk):
    n, d = x2d.shape
    d_ff = w_down.shape[0]
    kern = functools.partial(_ffn_kernel, d_ff=d_ff, chunk=chunk)
    const = lambda i: (0, 0)
    return pl.pallas_call(
        kern,
        grid=(n // tile,),
        in_specs=[
            pl.BlockSpec((tile, d), lambda i: (i, 0)),
            pl.BlockSpec((1, d), const),
            _resident(w_gate_up.shape),
            _resident(w_down.shape),
            pl.BlockSpec((1, d), const),
        ],
        out_specs=pl.BlockSpec((tile, d), lambda i: (i, 0)),
        out_shape=jax.ShapeDtypeStruct((n, d), _F32),
        scratch_shapes=[pltpu.VMEM((tile, d_ff), _BF16)],
        compiler_params=pltpu.CompilerParams(
            dimension_semantics=("arbitrary",), vmem_limit_bytes=VMEM_LIMIT),
        name="ffn",
    )(x2d, g, w_gate_up, w_down, g_final)


def _trunk(x, mem, p):
    bsz, seq, d = x.shape
    n_mem = mem.shape[1]
    row = lambda v: v.reshape(1, -1)
    x2d = x.reshape(bsz * seq, d)
    kv = _kv_proj(mem.reshape(bsz * n_mem, d), row(p["norm_mem_g"]), p["w_kv"], n_mem)
    x1 = _mixer(x2d, seq, row(p["norm_mix_g"]), p["w_in"], p["conv_a_w"], p["conv_b_w"],
                row(p["conv_b_bias"]), row(p["ln_b_g"]), row(p["ln_b_b"]), p["w_out"],
                min(MIXER_TILE, seq))
    x2 = _xattn(x1, seq, n_mem, row(p["norm_xattn_g"]), p["w_q"], kv, p["w_o"],
                min(XATTN_TILE, seq))
    y = _ffn(x2, row(p["norm_ffn_g"]), p["w_gate_up"], p["w_down"], row(p["norm_final_g"]),
             min(FFN_TILE, seq), FFN_CHUNK)
    return y.reshape(bsz, seq, d)


def kernel(x_prompt, x_sample, mem_prompt, mem_sample, norm_mix_g, w_in, conv_a_w, conv_b_w,
           conv_b_bias, ln_b_g, ln_b_b, w_out, norm_xattn_g, norm_mem_g, w_q, w_kv, w_o,
           norm_ffn_g, w_gate_up, w_down, norm_final_g):
    assert norm_mix_g.shape[0] == 1, "single-layer trunk"
    p = dict(
        norm_mix_g=norm_mix_g[0],
        w_in=_pack_w_in(w_in[0], conv_a_w.shape[2], conv_b_w.shape[2]).astype(_BF16),
        conv_a_w=conv_a_w[0],
        conv_b_w=conv_b_w[0], conv_b_bias=conv_b_bias[0], ln_b_g=ln_b_g[0], ln_b_b=ln_b_b[0],
        w_out=w_out[0].astype(_BF16), norm_xattn_g=norm_xattn_g[0], norm_mem_g=norm_mem_g[0],
        w_q=w_q[0].astype(_BF16), w_kv=w_kv[0].astype(_BF16), w_o=w_o[0].astype(_BF16),
        norm_ffn_g=norm_ffn_g[0], w_gate_up=w_gate_up[0].astype(_BF16),
        w_down=w_down[0].astype(_BF16), norm_final_g=norm_final_g)
    y_prompt = _trunk(x_prompt, mem_prompt, p)
    y_sample = _trunk(x_sample, mem_sample, p)
    return (y_prompt, y_sample)
```

```python
import functools

import jax
import jax.numpy as jnp
from jax import lax
from jax.experimental import pallas as pl
from jax.experimental.pallas import tpu as pltpu

RMS_EPS = 1e-6
LN_EPS = 1e-5
KERNEL_A = 3
KERNEL_B = 31
N_XHEADS = 4
HALO = 16
CONV_ROWS = 64
LANE = 128
MXU_COLS = 256
VMEM_LIMIT = 60 * 1024 * 1024
MIXER_TILE = 1024
XATTN_TILE = 1024
FFN_TILE = 1024
FFN_CHUNK = 256

_F32 = jnp.float32
_BF16 = jnp.bfloat16


def _rms(x, g):
    ms = jnp.mean(x * x, axis=-1, keepdims=True)
    return x * lax.rsqrt(ms + RMS_EPS) * g


def _dot(a, b):
    return jnp.dot(a, b, preferred_element_type=_F32)


def _sigmoid(x):
    return 1.0 / (1.0 + jnp.exp(-x))


def _resident(shape):
    return pl.BlockSpec(shape, lambda i: (0,) * len(shape), pipeline_mode=pl.Buffered(1))


def _mem_fold_kernel(mem_ref, g_ref, wkv_ref, wq_ref, wo_ref, qk_ref, vo_ref, *, head_dim):
    d = mem_ref.shape[1]
    hm = _rms(mem_ref[...], g_ref[...]).astype(_BF16)
    kv = _dot(hm, wkv_ref[...])
    k = kv[:, 0:d].astype(_BF16)
    v = kv[:, d:2 * d].astype(_BF16)
    n_mem = mem_ref.shape[0]
    for hd in range(N_XHEADS):
        cols = slice(hd * head_dim, (hd + 1) * head_dim)
        qk = lax.dot_general(wq_ref[:, cols], k[:, cols], (((1,), (1,)), ((), ())),
                             preferred_element_type=_F32)
        qk_ref[:, hd * n_mem:(hd + 1) * n_mem] = (qk * (head_dim ** -0.5)).astype(_BF16)
        vo_ref[hd * n_mem:(hd + 1) * n_mem, :] = _dot(v[:, cols], wo_ref[cols, :]).astype(_BF16)


def _mem_fold(mem2d, n_mem, g, w_kv, w_q, w_o):
    n, d = mem2d.shape
    bsz = n // n_mem
    head_dim = d // N_XHEADS
    assert head_dim & (head_dim - 1) == 0 and head_dim.bit_length() % 2 == 1, head_dim
    kern = functools.partial(_mem_fold_kernel, head_dim=head_dim)
    return pl.pallas_call(
        kern,
        grid=(bsz,),
        in_specs=[
            pl.BlockSpec((n_mem, d), lambda b: (b, 0)),
            pl.BlockSpec((1, d), lambda b: (0, 0)),
            _resident(w_kv.shape),
            _resident(w_q.shape),
            _resident(w_o.shape),
        ],
        out_specs=[
            pl.BlockSpec((d, N_XHEADS * n_mem), lambda b: (b, 0)),
            pl.BlockSpec((N_XHEADS * n_mem, d), lambda b: (b, 0)),
        ],
        out_shape=[
            jax.ShapeDtypeStruct((bsz * d, N_XHEADS * n_mem), _BF16),
            jax.ShapeDtypeStruct((bsz * N_XHEADS * n_mem, d), _BF16),
        ],
        compiler_params=pltpu.CompilerParams(
            dimension_semantics=("arbitrary",), vmem_limit_bytes=VMEM_LIMIT),
        name="mem_fold",
    )(mem2d, g, w_kv, w_q, w_o)


def _mixer_kernel(xm_ref, xp_ref, xn_ref, g_ref, w_in_ref, ca_ref, cb_ref, cbias_ref,
                  lng_ref, lnb_ref, w_out_ref, o_ref,
                  h_scr, cv_scr, u_scr, ab_scr, conv_scr, y_scr,
                  *, tile, tiles_per_seq, dca, dcb):
    i = pl.program_id(0)
    pos = i % tiles_per_seq
    rows = tile + 2 * HALO
    pad_a = KERNEL_A // 2
    pad_b = KERNEL_B // 2
    n_slab = dcb // LANE
    assert dca == dcb, "the emission schedule below pairs group-A and group-B slabs"

    g = g_ref[...]
    h_scr[0:HALO, :] = _rms(xp_ref[...], g).astype(_BF16)
    h_scr[HALO:HALO + tile, :] = _rms(xm_ref[...], g).astype(_BF16)
    h_scr[HALO + tile:rows, :] = _rms(xn_ref[...], g).astype(_BF16)

    keep_top = pos > 0
    keep_bot = pos < tiles_per_seq - 1

    def store_slab(scr, c, blk):
        scr[c, 0:HALO, :] = jnp.where(keep_top, blk[0:HALO], 0.0)
        scr[c, HALO:HALO + tile, :] = blk[HALO:HALO + tile]
        scr[c, HALO + tile:rows, :] = jnp.where(keep_bot, blk[HALO + tile:rows], 0.0)

    def project_b(j):
        r = _dot(h_scr[...], w_in_ref[j])
        store_slab(u_scr, j, r[:, 0:LANE] * _sigmoid(r[:, LANE:2 * LANE]))

    def project_cv(j):
        r = _dot(h_scr[...], w_in_ref[n_slab + j])
        store_slab(cv_scr, j, r[:, 0:LANE] * r[:, LANE:2 * LANE])

    def project_ab(j):
        ab_scr[:, j * MXU_COLS:(j + 1) * MXU_COLS] = _dot(h_scr[HALO:HALO + tile, :],
                                                         w_in_ref[2 * n_slab + j])

    def conv_b_slab(lc):
        lanes = slice(lc * LANE, (lc + 1) * LANE)
        for c in range(tile // CONV_ROWS):
            base = c * CONV_ROWS
            acc = jnp.zeros((CONV_ROWS, LANE), _F32)
            for k in range(KERNEL_B):
                acc = acc + cb_ref[k:k + 1, lanes] * u_scr[lc, pl.ds(base + HALO - pad_b + k, CONV_ROWS), :]
            conv_scr[base:base + CONV_ROWS, lanes] = acc + cbias_ref[:, lanes]

    def conv_a_slab(lc):
        lanes = slice(lc * LANE, (lc + 1) * LANE)
        conv_a = jnp.zeros((tile, LANE), _F32)
        for k in range(KERNEL_A):
            conv_a = conv_a + ca_ref[k:k + 1, lanes] * cv_scr[lc, pl.ds(HALO - pad_a + k, tile), :]
        y_scr[:, lanes] = (ab_scr[:, lanes] * conv_a).astype(_BF16)

    slabs_per_ab = MXU_COLS // LANE
    project_b(0)
    for j in range(n_slab):
        project_cv(j)
        if j % slabs_per_ab == 0:
            project_ab(j // slabs_per_ab)
        if j + 1 < n_slab:
            project_b(j + 1)
        conv_b_slab(j)
        conv_a_slab(j)

    for c in range(tile // CONV_ROWS):
        base = c * CONV_ROWS
        conv = conv_scr[base:base + CONV_ROWS, :]
        mu = jnp.mean(conv, axis=-1, keepdims=True)
        cen = conv - mu
        var = jnp.mean(cen * cen, axis=-1, keepdims=True)
        ln = cen * lax.rsqrt(var + LN_EPS) * lng_ref[...] + lnb_ref[...]
        y_scr[base:base + CONV_ROWS, dca:dca + dcb] = (ln * _sigmoid(ln)).astype(_BF16)

    o_ref[...] = xm_ref[...] + _dot(y_scr[...], w_out_ref[...])


def _pack_w_in(w_in, dca, dcb):
    d = w_in.shape[0]

    def pair(a, b):
        n = a.shape[1] // LANE
        return jnp.stack([a.reshape(d, n, LANE), b.reshape(d, n, LANE)], axis=2).reshape(d, -1)

    a_b, a_c, a_v = w_in[:, 0:dca], w_in[:, dca:2 * dca], w_in[:, 2 * dca:3 * dca]
    b_val, b_gate = w_in[:, 3 * dca:3 * dca + dcb], w_in[:, 3 * dca + dcb:3 * dca + 2 * dcb]
    packed = jnp.concatenate([pair(b_val, b_gate), pair(a_c, a_v), a_b], axis=1)
    return packed.reshape(d, -1, MXU_COLS).transpose(1, 0, 2)


def _mixer(x2d, seq, g, w_in, ca, cb, cbias, lng, lnb, w_out, tile):
    n, d = x2d.shape
    dca = ca.shape[1]
    dcb = cb.shape[1]
    tiles_per_seq = seq // tile
    hb = tile // HALO
    n_hb = n // HALO
    kern = functools.partial(_mixer_kernel, tile=tile, tiles_per_seq=tiles_per_seq,
                             dca=dca, dcb=dcb)
    const = lambda i: (0, 0)
    return pl.pallas_call(
        kern,
        grid=(n // tile,),
        in_specs=[
            pl.BlockSpec((tile, d), lambda i: (i, 0)),
            pl.BlockSpec((HALO, d), lambda i: (jnp.maximum(i * hb - 1, 0), 0)),
            pl.BlockSpec((HALO, d), lambda i: (jnp.minimum((i + 1) * hb, n_hb - 1), 0)),
            pl.BlockSpec((1, d), const),
            _resident(w_in.shape),
            pl.BlockSpec(ca.shape, const),
            pl.BlockSpec(cb.shape, const),
            pl.BlockSpec((1, dcb), const),
            pl.BlockSpec((1, dcb), const),
            pl.BlockSpec((1, dcb), const),
            _resident(w_out.shape),
        ],
        out_specs=pl.BlockSpec((tile, d), lambda i: (i, 0)),
        out_shape=jax.ShapeDtypeStruct((n, d), _F32),
        scratch_shapes=[
            pltpu.VMEM((tile + 2 * HALO, d), _BF16),
            pltpu.VMEM((dca // LANE, tile + 2 * HALO, LANE), _F32),
            pltpu.VMEM((dcb // LANE, tile + 2 * HALO, LANE), _F32),
            pltpu.VMEM((tile, dca), _F32),
            pltpu.VMEM((tile, dcb), _F32),
            pltpu.VMEM((tile, dca + dcb), _BF16),
        ],
        compiler_params=pltpu.CompilerParams(
            dimension_semantics=("arbitrary",), vmem_limit_bytes=VMEM_LIMIT),
        name="mixer",
    )(x2d, x2d, x2d, g, w_in, ca, cb, cbias, lng, lnb, w_out)


def _xattn_kernel(x_ref, g_ref, qk_ref, vo_ref, o_ref, h_scr, p_scr, *, n_mem):
    tile = x_ref.shape[0]
    h_scr[...] = _rms(x_ref[...], g_ref[...]).astype(_BF16)
    half = tile // 2
    for r0 in (0, half):
        rows = slice(r0, r0 + half)
        s = _dot(h_scr[rows, :], qk_ref[...])
        for hd in range(N_XHEADS):
            cols = slice(hd * n_mem, (hd + 1) * n_mem)
            sh = s[:, cols]
            e = jnp.exp(sh - jnp.max(sh, axis=-1, keepdims=True))
            p_scr[rows, cols] = (e * (1.0 / jnp.sum(e, axis=-1, keepdims=True))).astype(_BF16)
        o_ref[rows, :] = x_ref[rows, :] + _dot(p_scr[rows, :], vo_ref[...])


def _xattn(x2d, seq, n_mem, g, qk, vo, tile):
    n, d = x2d.shape
    tiles_per_seq = seq // tile
    kern = functools.partial(_xattn_kernel, n_mem=n_mem)
    const = lambda i: (0, 0)
    return pl.pallas_call(
        kern,
        grid=(n // tile,),
        in_specs=[
            pl.BlockSpec((tile, d), lambda i: (i, 0)),
            pl.BlockSpec((1, d), const),
            pl.BlockSpec((d, N_XHEADS * n_mem), lambda i: (i // tiles_per_seq, 0)),
            pl.BlockSpec((N_XHEADS * n_mem, d), lambda i: (i // tiles_per_seq, 0)),
        ],
        out_specs=pl.BlockSpec((tile, d), lambda i: (i, 0)),
        out_shape=jax.ShapeDtypeStruct((n, d), _F32),
        scratch_shapes=[pltpu.VMEM((tile, d), _BF16),
                        pltpu.VMEM((tile, N_XHEADS * n_mem), _BF16)],
        compiler_params=pltpu.CompilerParams(
            dimension_semantics=("arbitrary",), vmem_limit_bytes=VMEM_LIMIT),
        name="xattn",
    )(x2d, g, qk, vo)


def _ffn_kernel(x_ref, g_ref, wgu_ref, wd_ref, gf_ref, o_ref, a_scr, *, d_ff, chunk):
    x = x_ref[...]
    h = _rms(x, g_ref[...]).astype(_BF16)
    for c in range(d_ff // chunk):
        gate = _dot(h, wgu_ref[:, c * chunk:(c + 1) * chunk])
        up = _dot(h, wgu_ref[:, d_ff + c * chunk:d_ff + (c + 1) * chunk])
        a_scr[:, c * chunk:(c + 1) * chunk] = (gate * _sigmoid(gate) * up).astype(_BF16)
    y = x + _dot(a_scr[...], wd_ref[...])
    o_ref[...] = _rms(y, gf_ref[...])


def _ffn(x2d, g, w_gate_up, w_down, g_final, tile, chunk):
    n, d = x2d.shape
    d_ff = w_down.shape[0]
    kern = functools.partial(_ffn_kernel, d_ff=d_ff, chunk=chunk)
    const = lambda i: (0, 0)
    return pl.pallas_call(
        kern,
        grid=(n // tile,),
        in_specs=[
            pl.BlockSpec((tile, d), lambda i: (i, 0)),
            pl.BlockSpec((1, d), const),
            _resident(w_gate_up.shape),
            _resident(w_down.shape),
            pl.BlockSpec((1, d), const),
        ],
        out_specs=pl.BlockSpec((tile, d), lambda i: (i, 0)),
        out_shape=jax.ShapeDtypeStruct((n, d), _F32),
        scratch_shapes=[pltpu.VMEM((tile, d_ff), _BF16)],
        compiler_params=pltpu.CompilerParams(
            dimension_semantics=("arbitrary",), vmem_limit_bytes=VMEM_LIMIT),
        name="ffn",
    )(x2d, g, w_gate_up, w_down, g_final)


def _trunk(x, mem, p):
    bsz, seq, d = x.shape
    n_mem = mem.shape[1]
    row = lambda v: v.reshape(1, -1)
    x2d = x.reshape(bsz * seq, d)
    qk, vo = _mem_fold(mem.reshape(bsz * n_mem, d), n_mem, row(p["norm_mem_g"]), p["w_kv"],
                       p["w_q"], p["w_o"])
    x1 = _mixer(x2d, seq, row(p["norm_mix_g"]), p["w_in"], p["conv_a_w"], p["conv_b_w"],
                row(p["conv_b_bias"]), row(p["ln_b_g"]), row(p["ln_b_b"]), p["w_out"],
                min(MIXER_TILE, seq))
    x2 = _xattn(x1, seq, n_mem, row(p["norm_xattn_g"]), qk, vo, min(XATTN_TILE, seq))
    y = _ffn(x2, row(p["norm_ffn_g"]), p["w_gate_up"], p["w_down"], row(p["norm_final_g"]),
             min(FFN_TILE, seq), FFN_CHUNK)
    return y.reshape(bsz, seq, d)


def kernel(x_prompt, x_sample, mem_prompt, mem_sample, norm_mix_g, w_in, conv_a_w, conv_b_w,
           conv_b_bias, ln_b_g, ln_b_b, w_out, norm_xattn_g, norm_mem_g, w_q, w_kv, w_o,
           norm_ffn_g, w_gate_up, w_down, norm_final_g):
    assert norm_mix_g.shape[0] == 1, "single-layer trunk"
    p = dict(
        norm_mix_g=norm_mix_g[0],
        w_in=_pack_w_in(w_in[0], conv_a_w.shape[2], conv_b_w.shape[2]).astype(_BF16),
        conv_a_w=conv_a_w[0],
        conv_b_w=conv_b_w[0], conv_b_bias=conv_b_bias[0], ln_b_g=ln_b_g[0], ln_b_b=ln_b_b[0],
        w_out=w_out[0].astype(_BF16), norm_xattn_g=norm_xattn_g[0], norm_mem_g=norm_mem_g[0],
        w_q=w_q[0].astype(_BF16), w_kv=w_kv[0].astype(_BF16), w_o=w_o[0].astype(_BF16),
        norm_ffn_g=norm_ffn_g[0], w_gate_up=w_gate_up[0].astype(_BF16),
        w_down=w_down[0].astype(_BF16), norm_final_g=norm_final_g)
    y_prompt = _trunk(x_prompt, mem_prompt, p)
    y_sample = _trunk(x_sample, mem_sample, p)
    return (y_prompt, y_sample)
```

```python
import functools

import jax
import jax.numpy as jnp
from jax import lax
from jax.experimental import pallas as pl
from jax.experimental.pallas import tpu as pltpu

RMS_EPS = 1e-6
LN_EPS = 1e-5
KERNEL_A = 3
KERNEL_B = 31
N_XHEADS = 4
HALO = 16
CONV_ROWS = 64
LANE = 128
MXU_COLS = 256
VMEM_LIMIT = 60 * 1024 * 1024
MIXER_TILE = 1024
FFN_TILE = 1024
FFN_CHUNK = 256

_F32 = jnp.float32
_BF16 = jnp.bfloat16


def _rms(x, g):
    ms = jnp.mean(x * x, axis=-1, keepdims=True)
    return x * lax.rsqrt(ms + RMS_EPS) * g


def _dot(a, b):
    return jnp.dot(a, b, preferred_element_type=_F32)


def _sigmoid(x):
    return 1.0 / (1.0 + jnp.exp(-x))


def _resident(shape):
    return pl.BlockSpec(shape, lambda i: (0,) * len(shape), pipeline_mode=pl.Buffered(1))


def _mem_fold_kernel(mem_ref, g_ref, wkv_ref, wq_ref, wo_ref, qk_ref, vo_ref, *, head_dim):
    d = mem_ref.shape[1]
    hm = _rms(mem_ref[...], g_ref[...]).astype(_BF16)
    kv = _dot(hm, wkv_ref[...])
    k = kv[:, 0:d].astype(_BF16)
    v = kv[:, d:2 * d].astype(_BF16)
    n_mem = mem_ref.shape[0]
    for hd in range(N_XHEADS):
        cols = slice(hd * head_dim, (hd + 1) * head_dim)
        qk = lax.dot_general(wq_ref[:, cols], k[:, cols], (((1,), (1,)), ((), ())),
                             preferred_element_type=_F32)
        qk_ref[:, hd * n_mem:(hd + 1) * n_mem] = (qk * (head_dim ** -0.5)).astype(_BF16)
        vo_ref[hd * n_mem:(hd + 1) * n_mem, :] = _dot(v[:, cols], wo_ref[cols, :]).astype(_BF16)


def _mem_fold(mem2d, n_mem, g, w_kv, w_q, w_o):
    n, d = mem2d.shape
    bsz = n // n_mem
    head_dim = d // N_XHEADS
    assert head_dim & (head_dim - 1) == 0 and head_dim.bit_length() % 2 == 1, head_dim
    kern = functools.partial(_mem_fold_kernel, head_dim=head_dim)
    return pl.pallas_call(
        kern,
        grid=(bsz,),
        in_specs=[
            pl.BlockSpec((n_mem, d), lambda b: (b, 0)),
            pl.BlockSpec((1, d), lambda b: (0, 0)),
            _resident(w_kv.shape),
            _resident(w_q.shape),
            _resident(w_o.shape),
        ],
        out_specs=[
            pl.BlockSpec((d, N_XHEADS * n_mem), lambda b: (b, 0)),
            pl.BlockSpec((N_XHEADS * n_mem, d), lambda b: (b, 0)),
        ],
        out_shape=[
            jax.ShapeDtypeStruct((bsz * d, N_XHEADS * n_mem), _BF16),
            jax.ShapeDtypeStruct((bsz * N_XHEADS * n_mem, d), _BF16),
        ],
        compiler_params=pltpu.CompilerParams(
            dimension_semantics=("arbitrary",), vmem_limit_bytes=VMEM_LIMIT),
        name="mem_fold",
    )(mem2d, g, w_kv, w_q, w_o)


def _mixer_kernel(xm_ref, xp_ref, xn_ref, g_ref, w_in_ref, ca_ref, cb_ref, cbias_ref,
                  lng_ref, lnb_ref, w_out_ref, gx_ref, qk_ref, vo_ref, o_ref,
                  h_scr, cv_scr, u_scr, ab_scr, conv_scr, y_scr, x1_scr, hx_scr, p_scr,
                  *, tile, tiles_per_seq, dca, dcb, n_mem):
    i = pl.program_id(0)
    pos = i % tiles_per_seq
    rows = tile + 2 * HALO
    pad_a = KERNEL_A // 2
    pad_b = KERNEL_B // 2
    n_slab = dcb // LANE
    assert dca == dcb, "the emission schedule below pairs group-A and group-B slabs"

    g = g_ref[...]
    h_scr[0:HALO, :] = _rms(xp_ref[...], g).astype(_BF16)
    h_scr[HALO:HALO + tile, :] = _rms(xm_ref[...], g).astype(_BF16)
    h_scr[HALO + tile:rows, :] = _rms(xn_ref[...], g).astype(_BF16)

    keep_top = pos > 0
    keep_bot = pos < tiles_per_seq - 1

    def store_slab(scr, c, blk):
        scr[c, 0:HALO, :] = jnp.where(keep_top, blk[0:HALO], 0.0)
        scr[c, HALO:HALO + tile, :] = blk[HALO:HALO + tile]
        scr[c, HALO + tile:rows, :] = jnp.where(keep_bot, blk[HALO + tile:rows], 0.0)

    def w_in_chunk(c):
        return w_in_ref[:, c * MXU_COLS:(c + 1) * MXU_COLS]

    def project_b(j):
        r = _dot(h_scr[...], w_in_chunk(j))
        store_slab(u_scr, j, r[:, 0:LANE] * _sigmoid(r[:, LANE:2 * LANE]))

    def project_cv(j):
        r = _dot(h_scr[...], w_in_chunk(n_slab + j))
        store_slab(cv_scr, j, r[:, 0:LANE] * r[:, LANE:2 * LANE])

    def project_ab(j):
        ab_scr[:, j * MXU_COLS:(j + 1) * MXU_COLS] = _dot(h_scr[HALO:HALO + tile, :],
                                                         w_in_chunk(2 * n_slab + j))

    def conv_b_slab(lc):
        lanes = slice(lc * LANE, (lc + 1) * LANE)
        for c in range(tile // CONV_ROWS):
            base = c * CONV_ROWS
            acc = jnp.zeros((CONV_ROWS, LANE), _F32)
            for k in range(KERNEL_B):
                acc = acc + cb_ref[k:k + 1, lanes] * u_scr[lc, pl.ds(base + HALO - pad_b + k, CONV_ROWS), :]
            conv_scr[base:base + CONV_ROWS, lanes] = acc + cbias_ref[:, lanes]

    def conv_a_slab(lc):
        lanes = slice(lc * LANE, (lc + 1) * LANE)
        conv_a = jnp.zeros((tile, LANE), _F32)
        for k in range(KERNEL_A):
            conv_a = conv_a + ca_ref[k:k + 1, lanes] * cv_scr[lc, pl.ds(HALO - pad_a + k, tile), :]
        y_scr[:, lanes] = (ab_scr[:, lanes] * conv_a).astype(_BF16)

    slabs_per_ab = MXU_COLS // LANE
    project_b(0)
    for j in range(n_slab):
        project_cv(j)
        if j % slabs_per_ab == 0:
            project_ab(j // slabs_per_ab)
        if j + 1 < n_slab:
            project_b(j + 1)
        conv_b_slab(j)
        conv_a_slab(j)

    for c in range(tile // CONV_ROWS):
        base = c * CONV_ROWS
        conv = conv_scr[base:base + CONV_ROWS, :]
        mu = jnp.mean(conv, axis=-1, keepdims=True)
        cen = conv - mu
        var = jnp.mean(cen * cen, axis=-1, keepdims=True)
        ln = cen * lax.rsqrt(var + LN_EPS) * lng_ref[...] + lnb_ref[...]
        y_scr[base:base + CONV_ROWS, dca:dca + dcb] = (ln * _sigmoid(ln)).astype(_BF16)

    half = tile // 2
    for r0 in (0, half):
        part = slice(r0, r0 + half)
        x1 = xm_ref[part, :] + _dot(y_scr[part, :], w_out_ref[...])
        x1_scr[part, :] = x1
        hx_scr[part, :] = _rms(x1, gx_ref[...]).astype(_BF16)
    for r0 in (0, half):
        part = slice(r0, r0 + half)
        s = _dot(hx_scr[part, :], qk_ref[...])
        for hd in range(N_XHEADS):
            cols = slice(hd * n_mem, (hd + 1) * n_mem)
            sh = s[:, cols]
            e = jnp.exp(sh - jnp.max(sh, axis=-1, keepdims=True))
            p_scr[part, cols] = (e * (1.0 / jnp.sum(e, axis=-1, keepdims=True))).astype(_BF16)
        o_ref[part, :] = x1_scr[part, :] + _dot(p_scr[part, :], vo_ref[...])


def _pack_w_in(w_in, dca, dcb):
    d = w_in.shape[0]

    def pair(a, b):
        n = a.shape[1] // LANE
        return jnp.stack([a.reshape(d, n, LANE), b.reshape(d, n, LANE)], axis=2).reshape(d, -1)

    a_b, a_c, a_v = w_in[:, 0:dca], w_in[:, dca:2 * dca], w_in[:, 2 * dca:3 * dca]
    b_val, b_gate = w_in[:, 3 * dca:3 * dca + dcb], w_in[:, 3 * dca + dcb:3 * dca + 2 * dcb]
    return jnp.concatenate([pair(b_val, b_gate), pair(a_c, a_v), a_b], axis=1)


def _mixer(x2d, seq, g, w_in, ca, cb, cbias, lng, lnb, w_out, gx, qk, vo, n_mem, tile):
    n, d = x2d.shape
    dca = ca.shape[1]
    dcb = cb.shape[1]
    tiles_per_seq = seq // tile
    hb = tile // HALO
    n_hb = n // HALO
    kern = functools.partial(_mixer_kernel, tile=tile, tiles_per_seq=tiles_per_seq,
                             dca=dca, dcb=dcb, n_mem=n_mem)
    const = lambda i: (0, 0)
    return pl.pallas_call(
        kern,
        grid=(n // tile,),
        in_specs=[
            pl.BlockSpec((tile, d), lambda i: (i, 0)),
            pl.BlockSpec((HALO, d), lambda i: (jnp.maximum(i * hb - 1, 0), 0)),
            pl.BlockSpec((HALO, d), lambda i: (jnp.minimum((i + 1) * hb, n_hb - 1), 0)),
            pl.BlockSpec((1, d), const),
            _resident(w_in.shape),
            pl.BlockSpec(ca.shape, const),
            pl.BlockSpec(cb.shape, const),
            pl.BlockSpec((1, dcb), const),
            pl.BlockSpec((1, dcb), const),
            pl.BlockSpec((1, dcb), const),
            _resident(w_out.shape),
            pl.BlockSpec((1, d), const),
            pl.BlockSpec((d, N_XHEADS * n_mem), lambda i: (i // tiles_per_seq, 0)),
            pl.BlockSpec((N_XHEADS * n_mem, d), lambda i: (i // tiles_per_seq, 0)),
        ],
        out_specs=pl.BlockSpec((tile, d), lambda i: (i, 0)),
        out_shape=jax.ShapeDtypeStruct((n, d), _F32),
        scratch_shapes=[
            pltpu.VMEM((tile + 2 * HALO, d), _BF16),
            pltpu.VMEM((dca // LANE, tile + 2 * HALO, LANE), _F32),
            pltpu.VMEM((dcb // LANE, tile + 2 * HALO, LANE), _F32),
            pltpu.VMEM((tile, dca), _F32),
            pltpu.VMEM((tile, dcb), _F32),
            pltpu.VMEM((tile, dca + dcb), _BF16),
            pltpu.VMEM((tile, d), _F32),
            pltpu.VMEM((tile, d), _BF16),
            pltpu.VMEM((tile, N_XHEADS * n_mem), _BF16),
        ],
        compiler_params=pltpu.CompilerParams(
            dimension_semantics=("arbitrary",), vmem_limit_bytes=VMEM_LIMIT),
        name="mixer",
    )(x2d, x2d, x2d, g, w_in, ca, cb, cbias, lng, lnb, w_out, gx, qk, vo)


def _ffn_kernel(x_ref, g_ref, wgu_ref, wd_ref, gf_ref, o_ref, a_scr, *, d_ff, chunk):
    x = x_ref[...]
    h = _rms(x, g_ref[...]).astype(_BF16)
    for c in range(d_ff // chunk):
        gate = _dot(h, wgu_ref[:, c * chunk:(c + 1) * chunk])
        up = _dot(h, wgu_ref[:, d_ff + c * chunk:d_ff + (c + 1) * chunk])
        a_scr[:, c * chunk:(c + 1) * chunk] = (gate * _sigmoid(gate) * up).astype(_BF16)
    y = x + _dot(a_scr[...], wd_ref[...])
    o_ref[...] = _rms(y, gf_ref[...])


def _ffn(x2d, g, w_gate_up, w_down, g_final, tile, chunk):
    n, d = x2d.shape
    d_ff = w_down.shape[0]
    kern = functools.partial(_ffn_kernel, d_ff=d_ff, chunk=chunk)
    const = lambda i: (0, 0)
    return pl.pallas_call(
        kern,
        grid=(n // tile,),
        in_specs=[
            pl.BlockSpec((tile, d), lambda i: (i, 0)),
            pl.BlockSpec((1, d), const),
            _resident(w_gate_up.shape),
            _resident(w_down.shape),
            pl.BlockSpec((1, d), const),
        ],
        out_specs=pl.BlockSpec((tile, d), lambda i: (i, 0)),
        out_shape=jax.ShapeDtypeStruct((n, d), _F32),
        scratch_shapes=[pltpu.VMEM((tile, d_ff), _BF16)],
        compiler_params=pltpu.CompilerParams(
            dimension_semantics=("arbitrary",), vmem_limit_bytes=VMEM_LIMIT),
        name="ffn",
    )(x2d, g, w_gate_up, w_down, g_final)


def _trunk(x, mem, p):
    bsz, seq, d = x.shape
    n_mem = mem.shape[1]
    row = lambda v: v.reshape(1, -1)
    x2d = x.reshape(bsz * seq, d)
    qk, vo = _mem_fold(mem.reshape(bsz * n_mem, d), n_mem, row(p["norm_mem_g"]), p["w_kv"],
                       p["w_q"], p["w_o"])
    x2 = _mixer(x2d, seq, row(p["norm_mix_g"]), p["w_in"], p["conv_a_w"], p["conv_b_w"],
                row(p["conv_b_bias"]), row(p["ln_b_g"]), row(p["ln_b_b"]), p["w_out"],
                row(p["norm_xattn_g"]), qk, vo, n_mem, min(MIXER_TILE, seq))
    y = _ffn(x2, row(p["norm_ffn_g"]), p["w_gate_up"], p["w_down"], row(p["norm_final_g"]),
             min(FFN_TILE, seq), FFN_CHUNK)
    return y.reshape(bsz, seq, d)


def kernel(x_prompt, x_sample, mem_prompt, mem_sample, norm_mix_g, w_in, conv_a_w, conv_b_w,
           conv_b_bias, ln_b_g, ln_b_b, w_out, norm_xattn_g, norm_mem_g, w_q, w_kv, w_o,
           norm_ffn_g, w_gate_up, w_down, norm_final_g):
    assert norm_mix_g.shape[0] == 1, "single-layer trunk"
    p = dict(
        norm_mix_g=norm_mix_g[0],
        w_in=_pack_w_in(w_in[0], conv_a_w.shape[2], conv_b_w.shape[2]).astype(_BF16),
        conv_a_w=conv_a_w[0],
        conv_b_w=conv_b_w[0], conv_b_bias=conv_b_bias[0], ln_b_g=ln_b_g[0], ln_b_b=ln_b_b[0],
        w_out=w_out[0].astype(_BF16), norm_xattn_g=norm_xattn_g[0], norm_mem_g=norm_mem_g[0],
        w_q=w_q[0].astype(_BF16), w_kv=w_kv[0].astype(_BF16), w_o=w_o[0].astype(_BF16),
        norm_ffn_g=norm_ffn_g[0], w_gate_up=w_gate_up[0].astype(_BF16),
        w_down=w_down[0].astype(_BF16), norm_final_g=norm_final_g)
    y_prompt = _trunk(x_prompt, mem_prompt, p)
    y_sample = _trunk(x_sample, mem_sample, p)
    return (y_prompt, y_sample)
```

```python
import functools

import jax
import jax.numpy as jnp
from jax import lax
from jax.experimental import pallas as pl
from jax.experimental.pallas import tpu as pltpu

RMS_EPS = 1e-6
LN_EPS = 1e-5
KERNEL_A = 3
KERNEL_B = 31
N_XHEADS = 4
HALO = 16
CONV_ROWS = 64
LANE = 128
MXU_COLS = 256
VMEM_LIMIT = 60 * 1024 * 1024
MIXER_TILE = 1024
FFN_TILE = 1024
MEM_FOLD_SEQS = 2
FFN_CHUNK = 256

_F32 = jnp.float32
_BF16 = jnp.bfloat16


def _rms(x, g):
    ms = jnp.mean(x * x, axis=-1, keepdims=True)
    return x * lax.rsqrt(ms + RMS_EPS) * g


def _dot(a, b):
    return jnp.dot(a, b, preferred_element_type=_F32)


def _sigmoid(x):
    return 1.0 / (1.0 + jnp.exp(-x))


def _resident(shape):
    return pl.BlockSpec(shape, lambda i: (0,) * len(shape), pipeline_mode=pl.Buffered(1))


def _mem_fold_kernel(mem_ref, g_ref, wkv_ref, wq_ref, wo_ref, qk_ref, vo_ref, *, head_dim, n_mem):
    d = mem_ref.shape[1]
    hm = _rms(mem_ref[...], g_ref[...]).astype(_BF16)
    kv = _dot(hm, wkv_ref[...])
    width = N_XHEADS * n_mem
    for b in range(mem_ref.shape[0] // n_mem):
        k = kv[b * n_mem:(b + 1) * n_mem, 0:d].astype(_BF16)
        v = kv[b * n_mem:(b + 1) * n_mem, d:2 * d].astype(_BF16)
        for hd in range(N_XHEADS):
            cols = slice(hd * head_dim, (hd + 1) * head_dim)
            mems = slice(hd * n_mem, (hd + 1) * n_mem)
            qk = lax.dot_general(wq_ref[:, cols], k[:, cols], (((1,), (1,)), ((), ())),
                                 preferred_element_type=_F32)
            qk_ref[b * d:(b + 1) * d, mems] = (qk * (head_dim ** -0.5)).astype(_BF16)
            vo_ref[b * width + hd * n_mem:b * width + (hd + 1) * n_mem, :] = _dot(
                v[:, cols], wo_ref[cols, :]).astype(_BF16)


def _mem_fold(mem2d, n_mem, g, w_kv, w_q, w_o):
    n, d = mem2d.shape
    bsz = n // n_mem
    per_step = MEM_FOLD_SEQS if bsz % MEM_FOLD_SEQS == 0 else 1
    head_dim = d // N_XHEADS
    width = N_XHEADS * n_mem
    assert head_dim & (head_dim - 1) == 0 and head_dim.bit_length() % 2 == 1, head_dim
    kern = functools.partial(_mem_fold_kernel, head_dim=head_dim, n_mem=n_mem)
    return pl.pallas_call(
        kern,
        grid=(bsz // per_step,),
        in_specs=[
            pl.BlockSpec((per_step * n_mem, d), lambda b: (b, 0)),
            pl.BlockSpec((1, d), lambda b: (0, 0)),
            _resident(w_kv.shape),
            _resident(w_q.shape),
            _resident(w_o.shape),
        ],
        out_specs=[
            pl.BlockSpec((per_step * d, width), lambda b: (b, 0)),
            pl.BlockSpec((per_step * width, d), lambda b: (b, 0)),
        ],
        out_shape=[
            jax.ShapeDtypeStruct((bsz * d, width), _BF16),
            jax.ShapeDtypeStruct((bsz * width, d), _BF16),
        ],
        compiler_params=pltpu.CompilerParams(
            dimension_semantics=("arbitrary",), vmem_limit_bytes=VMEM_LIMIT),
        name="mem_fold",
    )(mem2d, g, w_kv, w_q, w_o)


def _mixer_kernel(xm_ref, xp_ref, xn_ref, g_ref, w_in_ref, ca_ref, cb_ref, cbias_ref,
                  lng_ref, lnb_ref, w_out_ref, gx_ref, qk_ref, vo_ref, o_ref,
                  h_scr, cv_scr, u_scr, ab_scr, conv_scr, y_scr, x1_scr, hx_scr, p_scr,
                  *, tile, tiles_per_seq, dca, dcb, n_mem):
    i = pl.program_id(0)
    pos = i % tiles_per_seq
    rows = tile + 2 * HALO
    pad_a = KERNEL_A // 2
    pad_b = KERNEL_B // 2
    n_slab = dcb // LANE
    assert dca == dcb, "the emission schedule below pairs group-A and group-B slabs"

    g = g_ref[...]
    h_scr[0:HALO, :] = _rms(xp_ref[...], g).astype(_BF16)
    h_scr[HALO:HALO + tile, :] = _rms(xm_ref[...], g).astype(_BF16)
    h_scr[HALO + tile:rows, :] = _rms(xn_ref[...], g).astype(_BF16)

    keep_top = pos > 0
    keep_bot = pos < tiles_per_seq - 1

    def store_slab(scr, c, blk):
        scr[c, 0:HALO, :] = jnp.where(keep_top, blk[0:HALO], 0.0)
        scr[c, HALO:HALO + tile, :] = blk[HALO:HALO + tile]
        scr[c, HALO + tile:rows, :] = jnp.where(keep_bot, blk[HALO + tile:rows], 0.0)

    def project_b(j):
        r = _dot(h_scr[...], w_in_ref[j])
        store_slab(u_scr, j, r[:, 0:LANE] * _sigmoid(r[:, LANE:2 * LANE]))

    def project_cv(j):
        r = _dot(h_scr[...], w_in_ref[n_slab + j])
        store_slab(cv_scr, j, r[:, 0:LANE] * r[:, LANE:2 * LANE])

    def project_ab(j):
        ab_scr[:, j * MXU_COLS:(j + 1) * MXU_COLS] = _dot(h_scr[HALO:HALO + tile, :],
                                                         w_in_ref[2 * n_slab + j])

    def conv_b_slab(lc):
        lanes = slice(lc * LANE, (lc + 1) * LANE)
        for c in range(tile // CONV_ROWS):
            base = c * CONV_ROWS
            acc = jnp.zeros((CONV_ROWS, LANE), _F32)
            for k in range(KERNEL_B):
                acc = acc + cb_ref[k:k + 1, lanes] * u_scr[lc, pl.ds(base + HALO - pad_b + k, CONV_ROWS), :]
            conv_scr[base:base + CONV_ROWS, lanes] = acc + cbias_ref[:, lanes]

    def conv_a_slab(lc):
        lanes = slice(lc * LANE, (lc + 1) * LANE)
        conv_a = jnp.zeros((tile, LANE), _F32)
        for k in range(KERNEL_A):
            conv_a = conv_a + ca_ref[k:k + 1, lanes] * cv_scr[lc, pl.ds(HALO - pad_a + k, tile), :]
        y_scr[:, lanes] = (ab_scr[:, lanes] * conv_a).astype(_BF16)

    slabs_per_ab = MXU_COLS // LANE
    project_b(0)
    for j in range(n_slab):
        project_cv(j)
        if j % slabs_per_ab == 0:
            project_ab(j // slabs_per_ab)
        if j + 1 < n_slab:
            project_b(j + 1)
        conv_b_slab(j)
        conv_a_slab(j)

    for c in range(tile // CONV_ROWS):
        base = c * CONV_ROWS
        conv = conv_scr[base:base + CONV_ROWS, :]
        mu = jnp.mean(conv, axis=-1, keepdims=True)
        cen = conv - mu
        var = jnp.mean(cen * cen, axis=-1, keepdims=True)
        ln = cen * lax.rsqrt(var + LN_EPS) * lng_ref[...] + lnb_ref[...]
        y_scr[base:base + CONV_ROWS, dca:dca + dcb] = (ln * _sigmoid(ln)).astype(_BF16)

    half = tile // 2
    for r0 in (0, half):
        part = slice(r0, r0 + half)
        x1 = xm_ref[part, :] + _dot(y_scr[part, :], w_out_ref[...])
        x1_scr[part, :] = x1
        hx_scr[part, :] = _rms(x1, gx_ref[...]).astype(_BF16)
    for r0 in (0, half):
        part = slice(r0, r0 + half)
        s = _dot(hx_scr[part, :], qk_ref[...])
        for hd in range(N_XHEADS):
            cols = slice(hd * n_mem, (hd + 1) * n_mem)
            sh = s[:, cols]
            e = jnp.exp(sh - jnp.max(sh, axis=-1, keepdims=True))
            p_scr[part, cols] = (e * (1.0 / jnp.sum(e, axis=-1, keepdims=True))).astype(_BF16)
        o_ref[part, :] = x1_scr[part, :] + _dot(p_scr[part, :], vo_ref[...])


def _pack_w_in(w_in, dca, dcb):
    d = w_in.shape[0]

    def pair(a, b):
        n = a.shape[1] // LANE
        return jnp.stack([a.reshape(d, n, LANE), b.reshape(d, n, LANE)], axis=2).reshape(d, -1)

    a_b, a_c, a_v = w_in[:, 0:dca], w_in[:, dca:2 * dca], w_in[:, 2 * dca:3 * dca]
    b_val, b_gate = w_in[:, 3 * dca:3 * dca + dcb], w_in[:, 3 * dca + dcb:3 * dca + 2 * dcb]
    packed = jnp.concatenate([pair(b_val, b_gate), pair(a_c, a_v), a_b], axis=1)
    return packed.reshape(d, -1, MXU_COLS).transpose(1, 0, 2)


def _mixer(x2d, seq, g, w_in, ca, cb, cbias, lng, lnb, w_out, gx, qk, vo, n_mem, tile):
    n, d = x2d.shape
    dca = ca.shape[1]
    dcb = cb.shape[1]
    tiles_per_seq = seq // tile
    hb = tile // HALO
    n_hb = n // HALO
    kern = functools.partial(_mixer_kernel, tile=tile, tiles_per_seq=tiles_per_seq,
                             dca=dca, dcb=dcb, n_mem=n_mem)
    const = lambda i: (0, 0)
    return pl.pallas_call(
        kern,
        grid=(n // tile,),
        in_specs=[
            pl.BlockSpec((tile, d), lambda i: (i, 0)),
            pl.BlockSpec((HALO, d), lambda i: (jnp.maximum(i * hb - 1, 0), 0)),
            pl.BlockSpec((HALO, d), lambda i: (jnp.minimum((i + 1) * hb, n_hb - 1), 0)),
            pl.BlockSpec((1, d), const),
            _resident(w_in.shape),
            pl.BlockSpec(ca.shape, const),
            pl.BlockSpec(cb.shape, const),
            pl.BlockSpec((1, dcb), const),
            pl.BlockSpec((1, dcb), const),
            pl.BlockSpec((1, dcb), const),
            _resident(w_out.shape),
            pl.BlockSpec((1, d), const),
            pl.BlockSpec((d, N_XHEADS * n_mem), lambda i: (i // tiles_per_seq, 0)),
            pl.BlockSpec((N_XHEADS * n_mem, d), lambda i: (i // tiles_per_seq, 0)),
        ],
        out_specs=pl.BlockSpec((tile, d), lambda i: (i, 0)),
        out_shape=jax.ShapeDtypeStruct((n, d), _F32),
        scratch_shapes=[
            pltpu.VMEM((tile + 2 * HALO, d), _BF16),
            pltpu.VMEM((dca // LANE, tile + 2 * HALO, LANE), _F32),
            pltpu.VMEM((dcb // LANE, tile + 2 * HALO, LANE), _F32),
            pltpu.VMEM((tile, dca), _F32),
            pltpu.VMEM((tile, dcb), _F32),
            pltpu.VMEM((tile, dca + dcb), _BF16),
            pltpu.VMEM((tile, d), _F32),
            pltpu.VMEM((tile, d), _BF16),
            pltpu.VMEM((tile, N_XHEADS * n_mem), _BF16),
        ],
        compiler_params=pltpu.CompilerParams(
            dimension_semantics=("arbitrary",), vmem_limit_bytes=VMEM_LIMIT),
        name="mixer",
    )(x2d, x2d, x2d, g, w_in, ca, cb, cbias, lng, lnb, w_out, gx, qk, vo)


def _ffn_kernel(x_ref, g_ref, wgu_ref, wd_ref, gf_ref, o_ref, h_scr, a_scr, *, d_ff, chunk):
    tile = x_ref.shape[0]
    half = tile // 2

    def swiglu(r0, r1, c):
        gate = _dot(h_scr[r0:r1, :], wgu_ref[:, c * chunk:(c + 1) * chunk])
        up = _dot(h_scr[r0:r1, :], wgu_ref[:, d_ff + c * chunk:d_ff + (c + 1) * chunk])
        a_scr[r0:r1, c * chunk:(c + 1) * chunk] = (gate * _sigmoid(gate) * up).astype(_BF16)

    for r0 in (0, half):
        h_scr[r0:r0 + half, :] = _rms(x_ref[r0:r0 + half, :], g_ref[...]).astype(_BF16)
        swiglu(r0, r0 + half, 0)
    for c in range(1, d_ff // chunk):
        swiglu(0, tile, c)
    for r0 in (0, half):
        y = x_ref[r0:r0 + half, :] + _dot(a_scr[r0:r0 + half, :], wd_ref[...])
        o_ref[r0:r0 + half, :] = _rms(y, gf_ref[...])


def _ffn(x2d, g, w_gate_up, w_down, g_final, tile, chunk):
    n, d = x2d.shape
    d_ff = w_down.shape[0]
    kern = functools.partial(_ffn_kernel, d_ff=d_ff, chunk=chunk)
    const = lambda i: (0, 0)
    return pl.pallas_call(
        kern,
        grid=(n // tile,),
        in_specs=[
            pl.BlockSpec((tile, d), lambda i: (i, 0)),
            pl.BlockSpec((1, d), const),
            _resident(w_gate_up.shape),
            _resident(w_down.shape),
            pl.BlockSpec((1, d), const),
        ],
        out_specs=pl.BlockSpec((tile, d), lambda i: (i, 0)),
        out_shape=jax.ShapeDtypeStruct((n, d), _F32),
        scratch_shapes=[pltpu.VMEM((tile, d), _BF16), pltpu.VMEM((tile, d_ff), _BF16)],
        compiler_params=pltpu.CompilerParams(
            dimension_semantics=("arbitrary",), vmem_limit_bytes=VMEM_LIMIT),
        name="ffn",
    )(x2d, g, w_gate_up, w_down, g_final)


def _trunk(x, mem, p):
    bsz, seq, d = x.shape
    n_mem = mem.shape[1]
    row = lambda v: v.reshape(1, -1)
    x2d = x.reshape(bsz * seq, d)
    qk, vo = _mem_fold(mem.reshape(bsz * n_mem, d), n_mem, row(p["norm_mem_g"]), p["w_kv"],
                       p["w_q"], p["w_o"])
    x2 = _mixer(x2d, seq, row(p["norm_mix_g"]), p["w_in"], p["conv_a_w"], p["conv_b_w"],
                row(p["conv_b_bias"]), row(p["ln_b_g"]), row(p["ln_b_b"]), p["w_out"],
                row(p["norm_xattn_g"]), qk, vo, n_mem, min(MIXER_TILE, seq))
    y = _ffn(x2, row(p["norm_ffn_g"]), p["w_gate_up"], p["w_down"], row(p["norm_final_g"]),
             min(FFN_TILE, seq), FFN_CHUNK)
    return y.reshape(bsz, seq, d)


def kernel(x_prompt, x_sample, mem_prompt, mem_sample, norm_mix_g, w_in, conv_a_w, conv_b_w,
           conv_b_bias, ln_b_g, ln_b_b, w_out, norm_xattn_g, norm_mem_g, w_q, w_kv, w_o,
           norm_ffn_g, w_gate_up, w_down, norm_final_g):
    assert norm_mix_g.shape[0] == 1, "single-layer trunk"
    p = dict(
        norm_mix_g=norm_mix_g[0],
        w_in=_pack_w_in(w_in[0], conv_a_w.shape[2], conv_b_w.shape[2]).astype(_BF16),
        conv_a_w=conv_a_w[0],
        conv_b_w=conv_b_w[0], conv_b_bias=conv_b_bias[0], ln_b_g=ln_b_g[0], ln_b_b=ln_b_b[0],
        w_out=w_out[0].astype(_BF16), norm_xattn_g=norm_xattn_g[0], norm_mem_g=norm_mem_g[0],
        w_q=w_q[0].astype(_BF16), w_kv=w_kv[0].astype(_BF16), w_o=w_o[0].astype(_BF16),
        norm_ffn_g=norm_ffn_g[0], w_gate_up=w_gate_up[0].astype(_BF16),
        w_down=w_down[0].astype(_BF16), norm_final_g=norm_final_g)
    y_prompt = _trunk(x_prompt, mem_prompt, p)
    y_sample = _trunk(x_sample, mem_sample, p)
    return (y_prompt, y_sample)
```

```python
import functools

import jax
import jax.numpy as jnp
from jax import lax
from jax.experimental import pallas as pl
from jax.experimental.pallas import tpu as pltpu

RMS_EPS = 1e-6
LN_EPS = 1e-5
KERNEL_A = 3
KERNEL_B = 31
N_XHEADS = 4
HALO = 16
CONV_ROWS = 64
LANE = 128
MXU_COLS = 256
VMEM_LIMIT = 60 * 1024 * 1024
MIXER_TILE = 1024
FFN_TILE = 1024
FFN_CHUNK = 256

_F32 = jnp.float32
_BF16 = jnp.bfloat16


def _rms(x, g):
    ms = jnp.mean(x * x, axis=-1, keepdims=True)
    return x * lax.rsqrt(ms + RMS_EPS) * g


def _dot(a, b):
    return jnp.dot(a, b, preferred_element_type=_F32)


def _sigmoid(x):
    return 1.0 / (1.0 + jnp.exp(-x))


def _resident(shape):
    return pl.BlockSpec(shape, lambda i: (0,) * len(shape), pipeline_mode=pl.Buffered(1))


def _mem_fold_kernel(mem_ref, g_ref, wkv_ref, wq_ref, wo_ref, qk_ref, vo_ref, *, head_dim):
    d = mem_ref.shape[1]
    hm = _rms(mem_ref[...], g_ref[...]).astype(_BF16)
    kv = _dot(hm, wkv_ref[...])
    k = kv[:, 0:d].astype(_BF16)
    v = kv[:, d:2 * d].astype(_BF16)
    n_mem = mem_ref.shape[0]
    for hd in range(N_XHEADS):
        cols = slice(hd * head_dim, (hd + 1) * head_dim)
        qk = lax.dot_general(wq_ref[:, cols], k[:, cols], (((1,), (1,)), ((), ())),
                             preferred_element_type=_F32)
        qk_ref[:, hd * n_mem:(hd + 1) * n_mem] = (qk * (head_dim ** -0.5)).astype(_BF16)
        vo_ref[hd * n_mem:(hd + 1) * n_mem, :] = _dot(v[:, cols], wo_ref[cols, :]).astype(_BF16)


def _mem_fold(mem2d, n_mem, g, w_kv, w_q, w_o):
    n, d = mem2d.shape
    bsz = n // n_mem
    head_dim = d // N_XHEADS
    assert head_dim & (head_dim - 1) == 0 and head_dim.bit_length() % 2 == 1, head_dim
    kern = functools.partial(_mem_fold_kernel, head_dim=head_dim)
    return pl.pallas_call(
        kern,
        grid=(bsz,),
        in_specs=[
            pl.BlockSpec((n_mem, d), lambda b: (b, 0)),
            pl.BlockSpec((1, d), lambda b: (0, 0)),
            _resident(w_kv.shape),
            _resident(w_q.shape),
            _resident(w_o.shape),
        ],
        out_specs=[
            pl.BlockSpec((d, N_XHEADS * n_mem), lambda b: (b, 0)),
            pl.BlockSpec((N_XHEADS * n_mem, d), lambda b: (b, 0)),
        ],
        out_shape=[
            jax.ShapeDtypeStruct((bsz * d, N_XHEADS * n_mem), _BF16),
            jax.ShapeDtypeStruct((bsz * N_XHEADS * n_mem, d), _BF16),
        ],
        compiler_params=pltpu.CompilerParams(
            dimension_semantics=("arbitrary",), vmem_limit_bytes=VMEM_LIMIT),
        name="mem_fold",
    )(mem2d, g, w_kv, w_q, w_o)


def _mixer_kernel(xm_ref, xp_ref, xn_ref, g_ref, w_in_ref, ca_ref, cb_ref, cbias_ref,
                  lng_ref, lnb_ref, w_out_ref, gx_ref, qk_ref, vo_ref, o_ref,
                  h_scr, cv_scr, u_scr, ab_scr, conv_scr, y_scr, x1_scr, hx_scr, p_scr,
                  *, tile, tiles_per_seq, dca, dcb, n_mem):
    i = pl.program_id(0)
    pos = i % tiles_per_seq
    rows = tile + 2 * HALO
    pad_a = KERNEL_A // 2
    pad_b = KERNEL_B // 2
    n_slab = dcb // LANE
    assert dca == dcb, "the emission schedule below pairs group-A and group-B slabs"

    g = g_ref[...]
    h_scr[0:HALO, :] = _rms(xp_ref[...], g).astype(_BF16)
    h_scr[HALO:HALO + tile, :] = _rms(xm_ref[...], g).astype(_BF16)
    h_scr[HALO + tile:rows, :] = _rms(xn_ref[...], g).astype(_BF16)

    keep_top = pos > 0
    keep_bot = pos < tiles_per_seq - 1

    def store_slab(scr, c, blk):
        scr[c, 0:HALO, :] = jnp.where(keep_top, blk[0:HALO], 0.0)
        scr[c, HALO:HALO + tile, :] = blk[HALO:HALO + tile]
        scr[c, HALO + tile:rows, :] = jnp.where(keep_bot, blk[HALO + tile:rows], 0.0)

    def project_b(j):
        r = _dot(h_scr[...], w_in_ref[j])
        store_slab(u_scr, j, r[:, 0:LANE] * _sigmoid(r[:, LANE:2 * LANE]))

    def project_cv(j):
        r = _dot(h_scr[...], w_in_ref[n_slab + j])
        store_slab(cv_scr, j, r[:, 0:LANE] * r[:, LANE:2 * LANE])

    def project_ab(j):
        ab_scr[:, j * MXU_COLS:(j + 1) * MXU_COLS] = _dot(h_scr[HALO:HALO + tile, :],
                                                         w_in_ref[2 * n_slab + j])

    def conv_b_slab(lc):
        lanes = slice(lc * LANE, (lc + 1) * LANE)
        for c in range(tile // CONV_ROWS):
            base = c * CONV_ROWS
            acc = jnp.zeros((CONV_ROWS, LANE), _F32)
            for k in range(KERNEL_B):
                acc = acc + cb_ref[k:k + 1, lanes] * u_scr[lc, pl.ds(base + HALO - pad_b + k, CONV_ROWS), :]
            conv_scr[base:base + CONV_ROWS, lanes] = acc + cbias_ref[:, lanes]

    def conv_a_slab(lc):
        lanes = slice(lc * LANE, (lc + 1) * LANE)
        conv_a = jnp.zeros((tile, LANE), _F32)
        for k in range(KERNEL_A):
            conv_a = conv_a + ca_ref[k:k + 1, lanes] * cv_scr[lc, pl.ds(HALO - pad_a + k, tile), :]
        y_scr[:, lanes] = (ab_scr[:, lanes] * conv_a).astype(_BF16)

    slabs_per_ab = MXU_COLS // LANE
    project_b(0)
    for j in range(n_slab):
        project_cv(j)
        if j % slabs_per_ab == 0:
            project_ab(j // slabs_per_ab)
        if j + 1 < n_slab:
            project_b(j + 1)
        conv_b_slab(j)
        conv_a_slab(j)

    for c in range(tile // CONV_ROWS):
        base = c * CONV_ROWS
        conv = conv_scr[base:base + CONV_ROWS, :]
        mu = jnp.mean(conv, axis=-1, keepdims=True)
        cen = conv - mu
        var = jnp.mean(cen * cen, axis=-1, keepdims=True)
        ln = cen * lax.rsqrt(var + LN_EPS) * lng_ref[...] + lnb_ref[...]
        y_scr[base:base + CONV_ROWS, dca:dca + dcb] = (ln * _sigmoid(ln)).astype(_BF16)

    half = tile // 2
    for r0 in (0, half):
        part = slice(r0, r0 + half)
        x1 = xm_ref[part, :] + _dot(y_scr[part, :], w_out_ref[...])
        x1_scr[part, :] = x1
        hx_scr[part, :] = _rms(x1, gx_ref[...]).astype(_BF16)
    for r0 in (0, half):
        part = slice(r0, r0 + half)
        s = _dot(hx_scr[part, :], qk_ref[...])
        for hd in range(N_XHEADS):
            cols = slice(hd * n_mem, (hd + 1) * n_mem)
            sh = s[:, cols]
            e = jnp.exp(sh - jnp.max(sh, axis=-1, keepdims=True))
            p_scr[part, cols] = (e * (1.0 / jnp.sum(e, axis=-1, keepdims=True))).astype(_BF16)
        o_ref[part, :] = x1_scr[part, :] + _dot(p_scr[part, :], vo_ref[...])


def _pack_w_in(w_in, dca, dcb):
    d = w_in.shape[0]

    def pair(a, b):
        n = a.shape[1] // LANE
        return jnp.stack([a.reshape(d, n, LANE), b.reshape(d, n, LANE)], axis=2).reshape(d, -1)

    a_b, a_c, a_v = w_in[:, 0:dca], w_in[:, dca:2 * dca], w_in[:, 2 * dca:3 * dca]
    b_val, b_gate = w_in[:, 3 * dca:3 * dca + dcb], w_in[:, 3 * dca + dcb:3 * dca + 2 * dcb]
    packed = jnp.concatenate([pair(b_val, b_gate), pair(a_c, a_v), a_b], axis=1)
    return packed.reshape(d, -1, MXU_COLS).transpose(1, 0, 2)


def _mixer(x2d, seq, g, w_in, ca, cb, cbias, lng, lnb, w_out, gx, qk, vo, n_mem, tile):
    n, d = x2d.shape
    dca = ca.shape[1]
    dcb = cb.shape[1]
    tiles_per_seq = seq // tile
    hb = tile // HALO
    n_hb = n // HALO
    kern = functools.partial(_mixer_kernel, tile=tile, tiles_per_seq=tiles_per_seq,
                             dca=dca, dcb=dcb, n_mem=n_mem)
    const = lambda i: (0, 0)
    return pl.pallas_call(
        kern,
        grid=(n // tile,),
        in_specs=[
            pl.BlockSpec((tile, d), lambda i: (i, 0)),
            pl.BlockSpec((HALO, d), lambda i: (jnp.maximum(i * hb - 1, 0), 0)),
            pl.BlockSpec((HALO, d), lambda i: (jnp.minimum((i + 1) * hb, n_hb - 1), 0)),
            pl.BlockSpec((1, d), const),
            _resident(w_in.shape),
            pl.BlockSpec(ca.shape, const),
            pl.BlockSpec(cb.shape, const),
            pl.BlockSpec((1, dcb), const),
            pl.BlockSpec((1, dcb), const),
            pl.BlockSpec((1, dcb), const),
            _resident(w_out.shape),
            pl.BlockSpec((1, d), const),
            pl.BlockSpec((d, N_XHEADS * n_mem), lambda i: (i // tiles_per_seq, 0)),
            pl.BlockSpec((N_XHEADS * n_mem, d), lambda i: (i // tiles_per_seq, 0)),
        ],
        out_specs=pl.BlockSpec((tile, d), lambda i: (i, 0)),
        out_shape=jax.ShapeDtypeStruct((n, d), _F32),
        scratch_shapes=[
            pltpu.VMEM((tile + 2 * HALO, d), _BF16),
            pltpu.VMEM((dca // LANE, tile + 2 * HALO, LANE), _F32),
            pltpu.VMEM((dcb // LANE, tile + 2 * HALO, LANE), _F32),
            pltpu.VMEM((tile, dca), _F32),
            pltpu.VMEM((tile, dcb), _F32),
            pltpu.VMEM((tile, dca + dcb), _BF16),
            pltpu.VMEM((tile, d), _F32),
            pltpu.VMEM((tile, d), _BF16),
            pltpu.VMEM((tile, N_XHEADS * n_mem), _BF16),
        ],
        compiler_params=pltpu.CompilerParams(
            dimension_semantics=("arbitrary",), vmem_limit_bytes=VMEM_LIMIT),
        name="mixer",
    )(x2d, x2d, x2d, g, w_in, ca, cb, cbias, lng, lnb, w_out, gx, qk, vo)


def _ffn_kernel(x_ref, g_ref, wgu_ref, wd_ref, gf_ref, o_ref, a_scr, *, d_ff, chunk):
    x = x_ref[...]
    h = _rms(x, g_ref[...]).astype(_BF16)
    for c in range(d_ff // chunk):
        gate = _dot(h, wgu_ref[:, c * chunk:(c + 1) * chunk])
        up = _dot(h, wgu_ref[:, d_ff + c * chunk:d_ff + (c + 1) * chunk])
        a_scr[:, c * chunk:(c + 1) * chunk] = (gate * _sigmoid(gate) * up).astype(_BF16)
    y = x + _dot(a_scr[...], wd_ref[...])
    o_ref[...] = _rms(y, gf_ref[...])


def _ffn(x2d, g, w_gate_up, w_down, g_final, tile, chunk):
    n, d = x2d.shape
    d_ff = w_down.shape[0]
    kern = functools.partial(_ffn_kernel, d_ff=d_ff, chunk=chunk)
    const = lambda i: (0, 0)
    return pl.pallas_call(
        kern,
        grid=(n // tile,),
        in_specs=[
            pl.BlockSpec((tile, d), lambda i: (i, 0)),
            pl.BlockSpec((1, d), const),
            _resident(w_gate_up.shape),
            _resident(w_down.shape),
            pl.BlockSpec((1, d), const),
        ],
        out_specs=pl.BlockSpec((tile, d), lambda i: (i, 0)),
        out_shape=jax.ShapeDtypeStruct((n, d), _F32),
        scratch_shapes=[pltpu.VMEM((tile, d_ff), _BF16)],
        compiler_params=pltpu.CompilerParams(
            dimension_semantics=("arbitrary",), vmem_limit_bytes=VMEM_LIMIT),
        name="ffn",
    )(x2d, g, w_gate_up, w_down, g_final)


def _trunk(x, mem, p):
    bsz, seq, d = x.shape
    n_mem = mem.shape[1]
    row = lambda v: v.reshape(1, -1)
    x2d = x.reshape(bsz * seq, d)
    qk, vo = _mem_fold(mem.reshape(bsz * n_mem, d), n_mem, row(p["norm_mem_g"]), p["w_kv"],
                       p["w_q"], p["w_o"])
    x2 = _mixer(x2d, seq, row(p["norm_mix_g"]), p["w_in"], p["conv_a_w"], p["conv_b_w"],
                row(p["conv_b_bias"]), row(p["ln_b_g"]), row(p["ln_b_b"]), p["w_out"],
                row(p["norm_xattn_g"]), qk, vo, n_mem, min(MIXER_TILE, seq))
    y = _ffn(x2, row(p["norm_ffn_g"]), p["w_gate_up"], p["w_down"], row(p["norm_final_g"]),
             min(FFN_TILE, seq), FFN_CHUNK)
    return y.reshape(bsz, seq, d)


def kernel(x_prompt, x_sample, mem_prompt, mem_sample, norm_mix_g, w_in, conv_a_w, conv_b_w,
           conv_b_bias, ln_b_g, ln_b_b, w_out, norm_xattn_g, norm_mem_g, w_q, w_kv, w_o,
           norm_ffn_g, w_gate_up, w_down, norm_final_g):
    assert norm_mix_g.shape[0] == 1, "single-layer trunk"
    p = dict(
        norm_mix_g=norm_mix_g[0],
        w_in=_pack_w_in(w_in[0], conv_a_w.shape[2], conv_b_w.shape[2]).astype(_BF16),
        conv_a_w=conv_a_w[0],
        conv_b_w=conv_b_w[0], conv_b_bias=conv_b_bias[0], ln_b_g=ln_b_g[0], ln_b_b=ln_b_b[0],
        w_out=w_out[0].astype(_BF16), norm_xattn_g=norm_xattn_g[0], norm_mem_g=norm_mem_g[0],
        w_q=w_q[0].astype(_BF16), w_kv=w_kv[0].astype(_BF16), w_o=w_o[0].astype(_BF16),
        norm_ffn_g=norm_ffn_g[0], w_gate_up=w_gate_up[0].astype(_BF16),
        w_down=w_down[0].astype(_BF16), norm_final_g=norm_final_g)
    y_prompt = _trunk(x_prompt, mem_prompt, p)
    y_sample = _trunk(x_sample, mem_sample, p)
    return (y_prompt, y_sample)
```

```python
import functools

import jax
import jax.numpy as jnp
from jax import lax
from jax.experimental import pallas as pl
from jax.experimental.pallas import tpu as pltpu

RMS_EPS = 1e-6
LN_EPS = 1e-5
KERNEL_A = 3
KERNEL_B = 31
N_XHEADS = 4
HALO = 16
CONV_ROWS = 64
LANE = 128
MXU_COLS = 256
VMEM_LIMIT = 60 * 1024 * 1024
MIXER_TILE = 1024
FFN_TILE = 1024
FFN_CHUNK = 256

_F32 = jnp.float32
_BF16 = jnp.bfloat16


def _rms(x, g):
    ms = jnp.mean(x * x, axis=-1, keepdims=True)
    return x * lax.rsqrt(ms + RMS_EPS) * g


def _dot(a, b):
    return jnp.dot(a, b, preferred_element_type=_F32)


def _sigmoid(x):
    return 1.0 / (1.0 + jnp.exp(-x))


def _resident(shape):
    return pl.BlockSpec(shape, lambda i: (0,) * len(shape), pipeline_mode=pl.Buffered(1))


def _mem_fold_kernel(mem_ref, g_ref, wkv_ref, wq_ref, wo_ref, qk_ref, vo_ref, *, head_dim):
    d = mem_ref.shape[1]
    hm = _rms(mem_ref[...], g_ref[...]).astype(_BF16)
    kv = _dot(hm, wkv_ref[...])
    k = kv[:, 0:d].astype(_BF16)
    v = kv[:, d:2 * d].astype(_BF16)
    n_mem = mem_ref.shape[0]
    for hd in range(N_XHEADS):
        cols = slice(hd * head_dim, (hd + 1) * head_dim)
        qk = lax.dot_general(wq_ref[:, cols], k[:, cols], (((1,), (1,)), ((), ())),
                             preferred_element_type=_F32)
        qk_ref[:, hd * n_mem:(hd + 1) * n_mem] = (qk * (head_dim ** -0.5)).astype(_BF16)
        vo_ref[hd * n_mem:(hd + 1) * n_mem, :] = _dot(v[:, cols], wo_ref[cols, :]).astype(_BF16)


def _mem_fold(mem2d, n_mem, g, w_kv, w_q, w_o):
    n, d = mem2d.shape
    bsz = n // n_mem
    head_dim = d // N_XHEADS
    assert head_dim & (head_dim - 1) == 0 and head_dim.bit_length() % 2 == 1, head_dim
    kern = functools.partial(_mem_fold_kernel, head_dim=head_dim)
    return pl.pallas_call(
        kern,
        grid=(bsz,),
        in_specs=[
            pl.BlockSpec((n_mem, d), lambda b: (b, 0)),
            pl.BlockSpec((1, d), lambda b: (0, 0)),
            _resident(w_kv.shape),
            _resident(w_q.shape),
            _resident(w_o.shape),
        ],
        out_specs=[
            pl.BlockSpec((d, N_XHEADS * n_mem), lambda b: (b, 0)),
            pl.BlockSpec((N_XHEADS * n_mem, d), lambda b: (b, 0)),
        ],
        out_shape=[
            jax.ShapeDtypeStruct((bsz * d, N_XHEADS * n_mem), _BF16),
            jax.ShapeDtypeStruct((bsz * N_XHEADS * n_mem, d), _BF16),
        ],
        compiler_params=pltpu.CompilerParams(
            dimension_semantics=("arbitrary",), vmem_limit_bytes=VMEM_LIMIT),
        name="mem_fold",
    )(mem2d, g, w_kv, w_q, w_o)


def _mixer_kernel(xm_ref, xp_ref, xn_ref, g_ref, w_in_ref, ca_ref, cb_ref, cbias_ref,
                  lng_ref, lnb_ref, w_out_ref, gx_ref, qk_ref, vo_ref, o_ref,
                  h_scr, cv_scr, u_scr, ab_scr, conv_scr, y_scr, x1_scr, hx_scr, p_scr,
                  *, tile, tiles_per_seq, dca, dcb, n_mem):
    i = pl.program_id(0)
    pos = i % tiles_per_seq
    rows = tile + 2 * HALO
    pad_a = KERNEL_A // 2
    pad_b = KERNEL_B // 2
    n_slab = dcb // LANE
    assert dca == dcb, "the emission schedule below pairs group-A and group-B slabs"

    g = g_ref[...]
    h_scr[0:HALO, :] = _rms(xp_ref[...], g).astype(_BF16)
    h_scr[HALO:HALO + tile, :] = _rms(xm_ref[...], g).astype(_BF16)
    h_scr[HALO + tile:rows, :] = _rms(xn_ref[...], g).astype(_BF16)

    keep_top = pos > 0
    keep_bot = pos < tiles_per_seq - 1

    def store_slab(scr, c, blk):
        scr[c, 0:HALO, :] = jnp.where(keep_top, blk[0:HALO], 0.0)
        scr[c, HALO:HALO + tile, :] = blk[HALO:HALO + tile]
        scr[c, HALO + tile:rows, :] = jnp.where(keep_bot, blk[HALO + tile:rows], 0.0)

    def project_b(j):
        r = _dot(h_scr[...], w_in_ref[j])
        store_slab(u_scr, j, r[:, 0:LANE] * _sigmoid(r[:, LANE:2 * LANE]))

    def project_cv(j):
        r = _dot(h_scr[...], w_in_ref[n_slab + j])
        store_slab(cv_scr, j, r[:, 0:LANE] * r[:, LANE:2 * LANE])

    def project_ab(j):
        ab_scr[:, j * MXU_COLS:(j + 1) * MXU_COLS] = _dot(h_scr[HALO:HALO + tile, :],
                                                         w_in_ref[2 * n_slab + j])

    def conv_b_slab(lc):
        lanes = slice(lc * LANE, (lc + 1) * LANE)
        for c in range(tile // CONV_ROWS):
            base = c * CONV_ROWS
            acc = jnp.zeros((CONV_ROWS, LANE), _F32)
            for k in range(KERNEL_B):
                acc = acc + cb_ref[k:k + 1, lanes] * u_scr[lc, pl.ds(base + HALO - pad_b + k, CONV_ROWS), :]
            conv_scr[base:base + CONV_ROWS, lanes] = acc + cbias_ref[:, lanes]

    def conv_a_slab(lc):
        lanes = slice(lc * LANE, (lc + 1) * LANE)
        conv_a = jnp.zeros((tile, LANE), _F32)
        for k in range(KERNEL_A):
            conv_a = conv_a + ca_ref[k:k + 1, lanes] * cv_scr[lc, pl.ds(HALO - pad_a + k, tile), :]
        y_scr[:, lanes] = (ab_scr[:, lanes] * conv_a).astype(_BF16)

    slabs_per_ab = MXU_COLS // LANE
    project_b(0)
    for j in range(n_slab):
        project_cv(j)
        if j % slabs_per_ab == 0:
            project_ab(j // slabs_per_ab)
        if j + 1 < n_slab:
            project_b(j + 1)
        conv_b_slab(j)
        conv_a_slab(j)

    for c in range(tile // CONV_ROWS):
        base = c * CONV_ROWS
        conv = conv_scr[base:base + CONV_ROWS, :]
        mu = jnp.mean(conv, axis=-1, keepdims=True)
        cen = conv - mu
        var = jnp.mean(cen * cen, axis=-1, keepdims=True)
        ln = cen * lax.rsqrt(var + LN_EPS) * lng_ref[...] + lnb_ref[...]
        y_scr[base:base + CONV_ROWS, dca:dca + dcb] = (ln * _sigmoid(ln)).astype(_BF16)

    half = tile // 2
    for r0 in (0, half):
        part = slice(r0, r0 + half)
        x1 = xm_ref[part, :] + _dot(y_scr[part, :], w_out_ref[...])
        x1_scr[part, :] = x1
        hx_scr[part, :] = _rms(x1, gx_ref[...]).astype(_BF16)
        s = _dot(hx_scr[part, :], qk_ref[...])
        for hd in range(N_XHEADS):
            cols = slice(hd * n_mem, (hd + 1) * n_mem)
            sh = s[:, cols]
            e = jnp.exp(sh - jnp.max(sh, axis=-1, keepdims=True))
            p_scr[part, cols] = (e * (1.0 / jnp.sum(e, axis=-1, keepdims=True))).astype(_BF16)
        o_ref[part, :] = x1_scr[part, :] + _dot(p_scr[part, :], vo_ref[...])


def _pack_w_in(w_in, dca, dcb):
    d = w_in.shape[0]

    def pair(a, b):
        n = a.shape[1] // LANE
        return jnp.stack([a.reshape(d, n, LANE), b.reshape(d, n, LANE)], axis=2).reshape(d, -1)

    a_b, a_c, a_v = w_in[:, 0:dca], w_in[:, dca:2 * dca], w_in[:, 2 * dca:3 * dca]
    b_val, b_gate = w_in[:, 3 * dca:3 * dca + dcb], w_in[:, 3 * dca + dcb:3 * dca + 2 * dcb]
    packed = jnp.concatenate([pair(b_val, b_gate), pair(a_c, a_v), a_b], axis=1)
    return packed.reshape(d, -1, MXU_COLS).transpose(1, 0, 2)


def _mixer(x2d, seq, g, w_in, ca, cb, cbias, lng, lnb, w_out, gx, qk, vo, n_mem, tile):
    n, d = x2d.shape
    dca = ca.shape[1]
    dcb = cb.shape[1]
    tiles_per_seq = seq // tile
    hb = tile // HALO
    n_hb = n // HALO
    kern = functools.partial(_mixer_kernel, tile=tile, tiles_per_seq=tiles_per_seq,
                             dca=dca, dcb=dcb, n_mem=n_mem)
    const = lambda i: (0, 0)
    return pl.pallas_call(
        kern,
        grid=(n // tile,),
        in_specs=[
            pl.BlockSpec((tile, d), lambda i: (i, 0)),
            pl.BlockSpec((HALO, d), lambda i: (jnp.maximum(i * hb - 1, 0), 0)),
            pl.BlockSpec((HALO, d), lambda i: (jnp.minimum((i + 1) * hb, n_hb - 1), 0)),
            pl.BlockSpec((1, d), const),
            _resident(w_in.shape),
            pl.BlockSpec(ca.shape, const),
            pl.BlockSpec(cb.shape, const),
            pl.BlockSpec((1, dcb), const),
            pl.BlockSpec((1, dcb), const),
            pl.BlockSpec((1, dcb), const),
            _resident(w_out.shape),
            pl.BlockSpec((1, d), const),
            pl.BlockSpec((d, N_XHEADS * n_mem), lambda i: (i // tiles_per_seq, 0)),
            pl.BlockSpec((N_XHEADS * n_mem, d), lambda i: (i // tiles_per_seq, 0)),
        ],
        out_specs=pl.BlockSpec((tile, d), lambda i: (i, 0)),
        out_shape=jax.ShapeDtypeStruct((n, d), _F32),
        scratch_shapes=[
            pltpu.VMEM((tile + 2 * HALO, d), _BF16),
            pltpu.VMEM((dca // LANE, tile + 2 * HALO, LANE), _F32),
            pltpu.VMEM((dcb // LANE, tile + 2 * HALO, LANE), _F32),
            pltpu.VMEM((tile, dca), _F32),
            pltpu.VMEM((tile, dcb), _F32),
            pltpu.VMEM((tile, dca + dcb), _BF16),
            pltpu.VMEM((tile, d), _F32),
            pltpu.VMEM((tile, d), _BF16),
            pltpu.VMEM((tile, N_XHEADS * n_mem), _BF16),
        ],
        compiler_params=pltpu.CompilerParams(
            dimension_semantics=("arbitrary",), vmem_limit_bytes=VMEM_LIMIT),
        name="mixer",
    )(x2d, x2d, x2d, g, w_in, ca, cb, cbias, lng, lnb, w_out, gx, qk, vo)


def _ffn_kernel(x_ref, g_ref, wgu_ref, wd_ref, gf_ref, o_ref, a_scr, *, d_ff, chunk):
    x = x_ref[...]
    h = _rms(x, g_ref[...]).astype(_BF16)
    for c in range(d_ff // chunk):
        gate = _dot(h, wgu_ref[:, c * chunk:(c + 1) * chunk])
        up = _dot(h, wgu_ref[:, d_ff + c * chunk:d_ff + (c + 1) * chunk])
        a_scr[:, c * chunk:(c + 1) * chunk] = (gate * _sigmoid(gate) * up).astype(_BF16)
    y = x + _dot(a_scr[...], wd_ref[...])
    o_ref[...] = _rms(y, gf_ref[...])


def _ffn(x2d, g, w_gate_up, w_down, g_final, tile, chunk):
    n, d = x2d.shape
    d_ff = w_down.shape[0]
    kern = functools.partial(_ffn_kernel, d_ff=d_ff, chunk=chunk)
    const = lambda i: (0, 0)
    return pl.pallas_call(
        kern,
        grid=(n // tile,),
        in_specs=[
            pl.BlockSpec((tile, d), lambda i: (i, 0)),
            pl.BlockSpec((1, d), const),
            _resident(w_gate_up.shape),
            _resident(w_down.shape),
            pl.BlockSpec((1, d), const),
        ],
        out_specs=pl.BlockSpec((tile, d), lambda i: (i, 0)),
        out_shape=jax.ShapeDtypeStruct((n, d), _F32),
        scratch_shapes=[pltpu.VMEM((tile, d_ff), _BF16)],
        compiler_params=pltpu.CompilerParams(
            dimension_semantics=("arbitrary",), vmem_limit_bytes=VMEM_LIMIT),
        name="ffn",
    )(x2d, g, w_gate_up, w_down, g_final)


def _trunk(x, mem, p):
    bsz, seq, d = x.shape
    n_mem = mem.shape[1]
    row = lambda v: v.reshape(1, -1)
    x2d = x.reshape(bsz * seq, d)
    qk, vo = _mem_fold(mem.reshape(bsz * n_mem, d), n_mem, row(p["norm_mem_g"]), p["w_kv"],
                       p["w_q"], p["w_o"])
    x2 = _mixer(x2d, seq, row(p["norm_mix_g"]), p["w_in"], p["conv_a_w"], p["conv_b_w"],
                row(p["conv_b_bias"]), row(p["ln_b_g"]), row(p["ln_b_b"]), p["w_out"],
                row(p["norm_xattn_g"]), qk, vo, n_mem, min(MIXER_TILE, seq))
    y = _ffn(x2, row(p["norm_ffn_g"]), p["w_gate_up"], p["w_down"], row(p["norm_final_g"]),
             min(FFN_TILE, seq), FFN_CHUNK)
    return y.reshape(bsz, seq, d)


def kernel(x_prompt, x_sample, mem_prompt, mem_sample, norm_mix_g, w_in, conv_a_w, conv_b_w,
           conv_b_bias, ln_b_g, ln_b_b, w_out, norm_xattn_g, norm_mem_g, w_q, w_kv, w_o,
           norm_ffn_g, w_gate_up, w_down, norm_final_g):
    assert norm_mix_g.shape[0] == 1, "single-layer trunk"
    p = dict(
        norm_mix_g=norm_mix_g[0],
        w_in=_pack_w_in(w_in[0], conv_a_w.shape[2], conv_b_w.shape[2]).astype(_BF16),
        conv_a_w=conv_a_w[0],
        conv_b_w=conv_b_w[0], conv_b_bias=conv_b_bias[0], ln_b_g=ln_b_g[0], ln_b_b=ln_b_b[0],
        w_out=w_out[0].astype(_BF16), norm_xattn_g=norm_xattn_g[0], norm_mem_g=norm_mem_g[0],
        w_q=w_q[0].astype(_BF16), w_kv=w_kv[0].astype(_BF16), w_o=w_o[0].astype(_BF16),
        norm_ffn_g=norm_ffn_g[0], w_gate_up=w_gate_up[0].astype(_BF16),
        w_down=w_down[0].astype(_BF16), norm_final_g=norm_final_g)
    y_prompt = _trunk(x_prompt, mem_prompt, p)
    y_sample = _trunk(x_sample, mem_sample, p)
    return (y_prompt, y_sample)
```

```python
import functools

import jax
import jax.numpy as jnp
from jax import lax
from jax.experimental import pallas as pl
from jax.experimental.pallas import tpu as pltpu

RMS_EPS = 1e-6
LN_EPS = 1e-5
KERNEL_A = 3
KERNEL_B = 31
N_XHEADS = 4
HALO = 16
CONV_ROWS = 64
LANE = 128
MXU_COLS = 256
VMEM_LIMIT = 60 * 1024 * 1024
MIXER_TILE = 1024
FFN_TILE = 1024
FFN_CHUNK = 256

_F32 = jnp.float32
_BF16 = jnp.bfloat16


def _rms(x, g):
    ms = jnp.mean(x * x, axis=-1, keepdims=True)
    return x * lax.rsqrt(ms + RMS_EPS) * g


def _dot(a, b):
    return jnp.dot(a, b, preferred_element_type=_F32)


def _sigmoid(x):
    return 1.0 / (1.0 + jnp.exp(-x))


def _resident(shape):
    return pl.BlockSpec(shape, lambda i: (0,) * len(shape), pipeline_mode=pl.Buffered(1))


def _mem_fold_kernel(mem_ref, g_ref, wkv_ref, wq_ref, wo_ref, qk_ref, vo_ref, *, head_dim):
    d = mem_ref.shape[1]
    hm = _rms(mem_ref[...], g_ref[...]).astype(_BF16)
    kv = _dot(hm, wkv_ref[...])
    k = kv[:, 0:d].astype(_BF16)
    v = kv[:, d:2 * d].astype(_BF16)
    n_mem = mem_ref.shape[0]
    for hd in range(N_XHEADS):
        cols = slice(hd * head_dim, (hd + 1) * head_dim)
        qk = lax.dot_general(wq_ref[:, cols], k[:, cols], (((1,), (1,)), ((), ())),
                             preferred_element_type=_F32)
        qk_ref[:, hd * n_mem:(hd + 1) * n_mem] = (qk * (head_dim ** -0.5)).astype(_BF16)
        vo_ref[hd * n_mem:(hd + 1) * n_mem, :] = _dot(v[:, cols], wo_ref[cols, :]).astype(_BF16)


def _mem_fold(mem2d, n_mem, g, w_kv, w_q, w_o):
    n, d = mem2d.shape
    bsz = n // n_mem
    head_dim = d // N_XHEADS
    assert head_dim & (head_dim - 1) == 0 and head_dim.bit_length() % 2 == 1, head_dim
    kern = functools.partial(_mem_fold_kernel, head_dim=head_dim)
    return pl.pallas_call(
        kern,
        grid=(bsz,),
        in_specs=[
            pl.BlockSpec((n_mem, d), lambda b: (b, 0)),
            pl.BlockSpec((1, d), lambda b: (0, 0)),
            _resident(w_kv.shape),
            _resident(w_q.shape),
            _resident(w_o.shape),
        ],
        out_specs=[
            pl.BlockSpec((d, N_XHEADS * n_mem), lambda b: (b, 0)),
            pl.BlockSpec((N_XHEADS * n_mem, d), lambda b: (b, 0)),
        ],
        out_shape=[
            jax.ShapeDtypeStruct((bsz * d, N_XHEADS * n_mem), _BF16),
            jax.ShapeDtypeStruct((bsz * N_XHEADS * n_mem, d), _BF16),
        ],
        compiler_params=pltpu.CompilerParams(
            dimension_semantics=("arbitrary",), vmem_limit_bytes=VMEM_LIMIT),
        name="mem_fold",
    )(mem2d, g, w_kv, w_q, w_o)


def _mixer_kernel(xm_ref, xp_ref, xn_ref, g_ref, w_in_ref, ca_ref, cb_ref, cbias_ref,
                  lng_ref, lnb_ref, w_out_ref, gx_ref, qk_ref, vo_ref, o_ref,
                  h_scr, cv_scr, u_scr, ab_scr, conv_scr, y_scr, x1_scr, hx_scr, p_scr,
                  *, tile, tiles_per_seq, dca, dcb, n_mem):
    i = pl.program_id(0)
    pos = i % tiles_per_seq
    rows = tile + 2 * HALO
    pad_a = KERNEL_A // 2
    pad_b = KERNEL_B // 2
    n_slab = dcb // LANE
    assert dca == dcb, "the emission schedule below pairs group-A and group-B slabs"

    g = g_ref[...]
    h_scr[0:HALO, :] = _rms(xp_ref[...], g).astype(_BF16)
    h_scr[HALO:HALO + tile, :] = _rms(xm_ref[...], g).astype(_BF16)
    h_scr[HALO + tile:rows, :] = _rms(xn_ref[...], g).astype(_BF16)

    keep_top = pos > 0
    keep_bot = pos < tiles_per_seq - 1

    def store_slab(scr, c, blk):
        scr[c, 0:HALO, :] = jnp.where(keep_top, blk[0:HALO], 0.0)
        scr[c, HALO:HALO + tile, :] = blk[HALO:HALO + tile]
        scr[c, HALO + tile:rows, :] = jnp.where(keep_bot, blk[HALO + tile:rows], 0.0)

    def project_b(j):
        r = _dot(h_scr[...], w_in_ref[j])
        store_slab(u_scr, j, r[:, 0:LANE] * _sigmoid(r[:, LANE:2 * LANE]))

    def project_cv(j):
        r = _dot(h_scr[...], w_in_ref[n_slab + j])
        store_slab(cv_scr, j, r[:, 0:LANE] * r[:, LANE:2 * LANE])

    def project_ab(j):
        ab_scr[:, j * MXU_COLS:(j + 1) * MXU_COLS] = _dot(h_scr[HALO:HALO + tile, :],
                                                         w_in_ref[2 * n_slab + j])

    def conv_b_slab(lc):
        lanes = slice(lc * LANE, (lc + 1) * LANE)
        for c in range(tile // CONV_ROWS):
            base = c * CONV_ROWS
            acc = jnp.zeros((CONV_ROWS, LANE), _F32)
            for k in range(KERNEL_B):
                acc = acc + cb_ref[k:k + 1, lanes] * u_scr[lc, pl.ds(base + HALO - pad_b + k, CONV_ROWS), :]
            conv_scr[base:base + CONV_ROWS, lanes] = acc + cbias_ref[:, lanes]

    def conv_a_slab(lc):
        lanes = slice(lc * LANE, (lc + 1) * LANE)
        conv_a = jnp.zeros((tile, LANE), _F32)
        for k in range(KERNEL_A):
            conv_a = conv_a + ca_ref[k:k + 1, lanes] * cv_scr[lc, pl.ds(HALO - pad_a + k, tile), :]
        y_scr[:, lanes] = (ab_scr[:, lanes] * conv_a).astype(_BF16)

    slabs_per_ab = MXU_COLS // LANE

    def project_a(j):
        project_cv(j)
        if j % slabs_per_ab == 0:
            project_ab(j // slabs_per_ab)

    project_b(0)
    project_b(1)
    project_a(0)
    for j in range(n_slab):
        if j + 2 < n_slab:
            project_b(j + 2)
        if j + 1 < n_slab:
            project_a(j + 1)
        conv_b_slab(j)
        conv_a_slab(j)

    for c in range(tile // CONV_ROWS):
        base = c * CONV_ROWS
        conv = conv_scr[base:base + CONV_ROWS, :]
        mu = jnp.mean(conv, axis=-1, keepdims=True)
        cen = conv - mu
        var = jnp.mean(cen * cen, axis=-1, keepdims=True)
        ln = cen * lax.rsqrt(var + LN_EPS) * lng_ref[...] + lnb_ref[...]
        y_scr[base:base + CONV_ROWS, dca:dca + dcb] = (ln * _sigmoid(ln)).astype(_BF16)

    half = tile // 2
    for r0 in (0, half):
        part = slice(r0, r0 + half)
        x1 = xm_ref[part, :] + _dot(y_scr[part, :], w_out_ref[...])
        x1_scr[part, :] = x1
        hx_scr[part, :] = _rms(x1, gx_ref[...]).astype(_BF16)
        s = _dot(hx_scr[part, :], qk_ref[...])
        for hd in range(N_XHEADS):
            cols = slice(hd * n_mem, (hd + 1) * n_mem)
            sh = s[:, cols]
            e = jnp.exp(sh - jnp.max(sh, axis=-1, keepdims=True))
            p_scr[part, cols] = (e * (1.0 / jnp.sum(e, axis=-1, keepdims=True))).astype(_BF16)
        o_ref[part, :] = x1_scr[part, :] + _dot(p_scr[part, :], vo_ref[...])


def _pack_w_in(w_in, dca, dcb):
    d = w_in.shape[0]

    def pair(a, b):
        n = a.shape[1] // LANE
        return jnp.stack([a.reshape(d, n, LANE), b.reshape(d, n, LANE)], axis=2).reshape(d, -1)

    a_b, a_c, a_v = w_in[:, 0:dca], w_in[:, dca:2 * dca], w_in[:, 2 * dca:3 * dca]
    b_val, b_gate = w_in[:, 3 * dca:3 * dca + dcb], w_in[:, 3 * dca + dcb:3 * dca + 2 * dcb]
    packed = jnp.concatenate([pair(b_val, b_gate), pair(a_c, a_v), a_b], axis=1)
    return packed.reshape(d, -1, MXU_COLS).transpose(1, 0, 2)


def _mixer(x2d, seq, g, w_in, ca, cb, cbias, lng, lnb, w_out, gx, qk, vo, n_mem, tile):
    n, d = x2d.shape
    dca = ca.shape[1]
    dcb = cb.shape[1]
    tiles_per_seq = seq // tile
    hb = tile // HALO
    n_hb = n // HALO
    kern = functools.partial(_mixer_kernel, tile=tile, tiles_per_seq=tiles_per_seq,
                             dca=dca, dcb=dcb, n_mem=n_mem)
    const = lambda i: (0, 0)
    return pl.pallas_call(
        kern,
        grid=(n // tile,),
        in_specs=[
            pl.BlockSpec((tile, d), lambda i: (i, 0)),
            pl.BlockSpec((HALO, d), lambda i: (jnp.maximum(i * hb - 1, 0), 0)),
            pl.BlockSpec((HALO, d), lambda i: (jnp.minimum((i + 1) * hb, n_hb - 1), 0)),
            pl.BlockSpec((1, d), const),
            _resident(w_in.shape),
            pl.BlockSpec(ca.shape, const),
            pl.BlockSpec(cb.shape, const),
            pl.BlockSpec((1, dcb), const),
            pl.BlockSpec((1, dcb), const),
            pl.BlockSpec((1, dcb), const),
            _resident(w_out.shape),
            pl.BlockSpec((1, d), const),
            pl.BlockSpec((d, N_XHEADS * n_mem), lambda i: (i // tiles_per_seq, 0)),
            pl.BlockSpec((N_XHEADS * n_mem, d), lambda i: (i // tiles_per_seq, 0)),
        ],
        out_specs=pl.BlockSpec((tile, d), lambda i: (i, 0)),
        out_shape=jax.ShapeDtypeStruct((n, d), _F32),
        scratch_shapes=[
            pltpu.VMEM((tile + 2 * HALO, d), _BF16),
            pltpu.VMEM((dca // LANE, tile + 2 * HALO, LANE), _F32),
            pltpu.VMEM((dcb // LANE, tile + 2 * HALO, LANE), _F32),
            pltpu.VMEM((tile, dca), _F32),
            pltpu.VMEM((tile, dcb), _F32),
            pltpu.VMEM((tile, dca + dcb), _BF16),
            pltpu.VMEM((tile, d), _F32),
            pltpu.VMEM((tile, d), _BF16),
            pltpu.VMEM((tile, N_XHEADS * n_mem), _BF16),
        ],
        compiler_params=pltpu.CompilerParams(
            dimension_semantics=("arbitrary",), vmem_limit_bytes=VMEM_LIMIT),
        name="mixer",
    )(x2d, x2d, x2d, g, w_in, ca, cb, cbias, lng, lnb, w_out, gx, qk, vo)


def _ffn_kernel(x_ref, g_ref, wgu_ref, wd_ref, gf_ref, o_ref, a_scr, *, d_ff, chunk):
    x = x_ref[...]
    h = _rms(x, g_ref[...]).astype(_BF16)
    for c in range(d_ff // chunk):
        gate = _dot(h, wgu_ref[:, c * chunk:(c + 1) * chunk])
        up = _dot(h, wgu_ref[:, d_ff + c * chunk:d_ff + (c + 1) * chunk])
        a_scr[:, c * chunk:(c + 1) * chunk] = (gate * _sigmoid(gate) * up).astype(_BF16)
    y = x + _dot(a_scr[...], wd_ref[...])
    o_ref[...] = _rms(y, gf_ref[...])


def _ffn(x2d, g, w_gate_up, w_down, g_final, tile, chunk):
    n, d = x2d.shape
    d_ff = w_down.shape[0]
    kern = functools.partial(_ffn_kernel, d_ff=d_ff, chunk=chunk)
    const = lambda i: (0, 0)
    return pl.pallas_call(
        kern,
        grid=(n // tile,),
        in_specs=[
            pl.BlockSpec((tile, d), lambda i: (i, 0)),
            pl.BlockSpec((1, d), const),
            _resident(w_gate_up.shape),
            _resident(w_down.shape),
            pl.BlockSpec((1, d), const),
        ],
        out_specs=pl.BlockSpec((tile, d), lambda i: (i, 0)),
        out_shape=jax.ShapeDtypeStruct((n, d), _F32),
        scratch_shapes=[pltpu.VMEM((tile, d_ff), _BF16)],
        compiler_params=pltpu.CompilerParams(
            dimension_semantics=("arbitrary",), vmem_limit_bytes=VMEM_LIMIT),
        name="ffn",
    )(x2d, g, w_gate_up, w_down, g_final)


def _trunk(x, mem, p):
    bsz, seq, d = x.shape
    n_mem = mem.shape[1]
    row = lambda v: v.reshape(1, -1)
    x2d = x.reshape(bsz * seq, d)
    qk, vo = _mem_fold(mem.reshape(bsz * n_mem, d), n_mem, row(p["norm_mem_g"]), p["w_kv"],
                       p["w_q"], p["w_o"])
    x2 = _mixer(x2d, seq, row(p["norm_mix_g"]), p["w_in"], p["conv_a_w"], p["conv_b_w"],
                row(p["conv_b_bias"]), row(p["ln_b_g"]), row(p["ln_b_b"]), p["w_out"],
                row(p["norm_xattn_g"]), qk, vo, n_mem, min(MIXER_TILE, seq))
    y = _ffn(x2, row(p["norm_ffn_g"]), p["w_gate_up"], p["w_down"], row(p["norm_final_g"]),
             min(FFN_TILE, seq), FFN_CHUNK)
    return y.reshape(bsz, seq, d)


def kernel(x_prompt, x_sample, mem_prompt, mem_sample, norm_mix_g, w_in, conv_a_w, conv_b_w,
           conv_b_bias, ln_b_g, ln_b_b, w_out, norm_xattn_g, norm_mem_g, w_q, w_kv, w_o,
           norm_ffn_g, w_gate_up, w_down, norm_final_g):
    assert norm_mix_g.shape[0] == 1, "single-layer trunk"
    p = dict(
        norm_mix_g=norm_mix_g[0],
        w_in=_pack_w_in(w_in[0], conv_a_w.shape[2], conv_b_w.shape[2]).astype(_BF16),
        conv_a_w=conv_a_w[0],
        conv_b_w=conv_b_w[0], conv_b_bias=conv_b_bias[0], ln_b_g=ln_b_g[0], ln_b_b=ln_b_b[0],
        w_out=w_out[0].astype(_BF16), norm_xattn_g=norm_xattn_g[0], norm_mem_g=norm_mem_g[0],
        w_q=w_q[0].astype(_BF16), w_kv=w_kv[0].astype(_BF16), w_o=w_o[0].astype(_BF16),
        norm_ffn_g=norm_ffn_g[0], w_gate_up=w_gate_up[0].astype(_BF16),
        w_down=w_down[0].astype(_BF16), norm_final_g=norm_final_g)
    y_prompt = _trunk(x_prompt, mem_prompt, p)
    y_sample = _trunk(x_sample, mem_sample, p)
    return (y_prompt, y_sample)
```
